```python
import math
import jax, jax.numpy as jnp
from jax import lax
import numpy as np

D_MODEL = 1024
BATCH = 8
SEQ = 4096
DEPTH = 1
DEC_BATCH = 128
DEC_SEQ = 8
PAST_LEN = 8192
PAGE_SIZE = 128

ATTN_WIDTH = D_MODEL // 2
POOL_WIDTH = D_MODEL - ATTN_WIDTH
N_HEADS = 4
HEAD_DIM = ATTN_WIDTH // (2 * N_HEADS)
POOL_WINDOWS = (2, 4, 8, 16)
N_POOL_GROUPS = len(POOL_WINDOWS)
POOL_GROUP = POOL_WIDTH // N_POOL_GROUPS
POOL_BUF = max(POOL_WINDOWS) - 1
IN_COLS = POOL_WIDTH + 3 * ATTN_WIDTH
D_FF = 4 * D_MODEL
ROPE_THETA = 10000.0
EPS = 1e-6
Q_BLOCK = 128
NEG_INF = -1e30
SCALE = HEAD_DIM ** -0.5

kernel_name = 'hymba_pool_diffattn_adaln_step'


def rmsnorm(x, g):
    xf = x.astype(jnp.float32)
    y = xf * lax.rsqrt(jnp.mean(xf * xf, axis=-1, keepdims=True) + EPS)
    return (y * g.astype(jnp.float32)).astype(x.dtype)


def modulation(c, w, b):
    m = jax.nn.silu(c) @ w + b
    return [t[:, None, :] for t in jnp.split(m, 6, axis=-1)]


def rope(x, pos):
    half = HEAD_DIM // 2
    inv_freq = ROPE_THETA ** (-jnp.arange(half, dtype=jnp.float32) * 2.0 / HEAD_DIM)
    ang = pos.astype(jnp.float32)[:, None] * inv_freq[None, :]
    cos = jnp.cos(ang)[:, None, :]
    sin = jnp.sin(ang)[:, None, :]
    xf = x.astype(jnp.float32)
    x1, x2 = xf[..., :half], xf[..., half:]
    return jnp.concatenate([x1 * cos - x2 * sin, x2 * cos + x1 * sin], axis=-1).astype(x.dtype)


def mixer_inputs(x, shift, scale, g, w_in, pos):
    B, L = x.shape[0], x.shape[1]
    h = rmsnorm(x, g) * (1.0 + scale) + shift
    u = h @ w_in
    u_pool = u[..., :POOL_WIDTH]
    q = u[..., POOL_WIDTH:POOL_WIDTH + ATTN_WIDTH].reshape(B, L, 2 * N_HEADS, HEAD_DIM)
    k = u[..., POOL_WIDTH + ATTN_WIDTH:POOL_WIDTH + 2 * ATTN_WIDTH].reshape(B, L, 2 * N_HEADS, HEAD_DIM)
    v = u[..., POOL_WIDTH + 2 * ATTN_WIDTH:].reshape(B, L, N_HEADS, 2 * HEAD_DIM)
    return u_pool, rope(q, pos), rope(k, pos), v


def pool_mix(u_ext, pos0, pool_w, pool_scale):
    B, E, C = u_ext.shape
    L = E - POOL_BUF
    cs = jnp.concatenate([jnp.zeros((B, 1, C), jnp.float32), jnp.cumsum(u_ext.astype(jnp.float32), axis=1)], axis=1)
    pos = pos0 + jnp.arange(L, dtype=jnp.int32)
    u = u_ext[:, POOL_BUF:].astype(jnp.float32)
    outs = []
    for g, w in enumerate(POOL_WINDOWS):
        sl = slice(g * POOL_GROUP, (g + 1) * POOL_GROUP)
        hi = cs[:, POOL_BUF + 1:POOL_BUF + 1 + L, sl]
        lo = cs[:, POOL_BUF + 1 - w:POOL_BUF + 1 - w + L, sl]
        cnt = jnp.minimum(w, pos + 1).astype(jnp.float32)[None, :, None]
        outs.append((hi - lo) / cnt - u[..., sl])
    d = jnp.stack(outs, axis=2)
    y = jnp.einsum('blgc,gcd->blgd', d, pool_w.astype(jnp.float32)).reshape(B, L, POOL_WIDTH)
    return (y * pool_scale.astype(jnp.float32)).astype(u_ext.dtype)


def split_comp(t):
    return t.reshape(t.shape[0], t.shape[1], N_HEADS, 2, HEAD_DIM).astype(jnp.float32)


def diff_weights(s, lam):
    p = jax.nn.softmax(s, axis=-1)
    return p[:, :, 0] - lam * p[:, :, 1]


def attend_prompt(q, k, v, lam):
    B, S = q.shape[0], q.shape[1]
    nb = S // Q_BLOCK
    kf = split_comp(k)
    vf = v.astype(jnp.float32)
    k_pos = jnp.arange(S, dtype=jnp.int32)
    qb = split_comp(q).reshape(B, nb, Q_BLOCK, N_HEADS, 2, HEAD_DIM).transpose(1, 0, 2, 3, 4, 5)
    pb = k_pos.reshape(nb, Q_BLOCK)

    def block(args):
        qi, pi = args
        s = jnp.einsum('bqhcd,bkhcd->bhcqk', qi, kf) * SCALE
        s = jnp.where((k_pos[None, :] <= pi[:, None])[None, None, None], s, NEG_INF)
        return jnp.einsum('bhqk,bkhe->bqhe', diff_weights(s, lam), vf)

    o = lax.map(block, (qb, pb))
    return o.transpose(1, 0, 2, 3, 4).reshape(B, S, N_HEADS, 2 * HEAD_DIM)


def attend_sample(q, k_new, v_new, k_past, v_past, lam):
    L = q.shape[1]
    P = k_past.shape[1]
    qf = split_comp(q)
    s_past = jnp.einsum('bqhcd,bkhcd->bhcqk', qf, split_comp(k_past)) * SCALE
    s_new = jnp.einsum('bqhcd,bkhcd->bhcqk', qf, split_comp(k_new)) * SCALE
    causal = jnp.arange(L)[None, :] <= jnp.arange(L)[:, None]
    s_new = jnp.where(causal[None, None, None], s_new, NEG_INF)
    p = diff_weights(jnp.concatenate([s_past, s_new], axis=-1), lam)
    return (jnp.einsum('bhqk,bkhe->bqhe', p[..., :P], v_past.astype(jnp.float32))
            + jnp.einsum('bhqk,bkhe->bqhe', p[..., P:], v_new.astype(jnp.float32)))


def head_norm(o, g, lam_init, dtype):
    return (rmsnorm(o, g) * (1.0 - lam_init)).astype(dtype)


def mixer_output(x, y_pool, o, gate1, shift2, scale2, gate2, w_out, g2, w1, w2):
    B, L = x.shape[0], x.shape[1]
    mix = jnp.concatenate([y_pool, o.reshape(B, L, ATTN_WIDTH)], axis=-1) @ w_out
    x = x + gate1 * mix
    h = rmsnorm(x, g2) * (1.0 + scale2) + shift2
    f = jnp.square(jax.nn.relu(h @ w1)) @ w2
    return x + gate2 * f


def setup_inputs(seed: int = 0) -> dict:
    key = jax.random.key(seed)
    ks = jax.random.split(key, 26)
    n_pages = PAST_LEN // PAGE_SIZE
    n_used = DEC_BATCH * n_pages
    n_pool_pages = (n_used * 5) // 4
    f32 = jnp.float32
    nrm = lambda k, shape, s: jax.random.normal(k, shape, f32) * s
    page_table = jax.random.permutation(ks[7], n_pool_pages)[:n_used].reshape(DEC_BATCH, n_pages).astype(jnp.int32)
    return {
        'x_prompt': nrm(ks[0], (BATCH, SEQ, D_MODEL), 1.0),
        'x_sample': nrm(ks[1], (DEC_BATCH, DEC_SEQ, D_MODEL), 1.0),
        'c_prompt': nrm(ks[2], (BATCH, D_MODEL), 1.0),
        'c_sample': nrm(ks[3], (DEC_BATCH, D_MODEL), 1.0),
        'cache_k': nrm(ks[4], (DEPTH, n_pool_pages, PAGE_SIZE, 2 * N_HEADS, HEAD_DIM), 1.0),
        'cache_v': nrm(ks[5], (DEPTH, n_pool_pages, PAGE_SIZE, N_HEADS, 2 * HEAD_DIM), 1.0),
        'state_pool': nrm(ks[6], (DEPTH, DEC_BATCH, POOL_BUF, POOL_WIDTH), 1.0),
        'page_table': page_table,
        'w_ada': nrm(ks[8], (DEPTH, D_MODEL, 6 * D_MODEL), 0.5 * D_MODEL ** -0.5),
        'b_ada': nrm(ks[9], (DEPTH, 6 * D_MODEL), 0.01),
        'norm1_g': 1.0 + nrm(ks[10], (DEPTH, D_MODEL), 0.01),
        'norm2_g': 1.0 + nrm(ks[11], (DEPTH, D_MODEL), 0.01),
        'w_in': nrm(ks[12], (DEPTH, D_MODEL, IN_COLS), D_MODEL ** -0.5),
        'pool_w': nrm(ks[13], (DEPTH, N_POOL_GROUPS, POOL_GROUP, POOL_GROUP), POOL_GROUP ** -0.5),
        'pool_scale': 1.0 + nrm(ks[14], (DEPTH, POOL_WIDTH), 0.01),
        'lambda_q1': nrm(ks[15], (DEPTH, HEAD_DIM), 0.1),
        'lambda_k1': nrm(ks[16], (DEPTH, HEAD_DIM), 0.1),
        'lambda_q2': nrm(ks[17], (DEPTH, HEAD_DIM), 0.1),
        'lambda_k2': nrm(ks[18], (DEPTH, HEAD_DIM), 0.1),
        'subln_g': 1.0 + nrm(ks[19], (DEPTH, 2 * HEAD_DIM), 0.01),
        'w_out': nrm(ks[20], (DEPTH, D_MODEL, D_MODEL), D_MODEL ** -0.5),
        'w_mlp1': nrm(ks[21], (DEPTH, D_MODEL, D_FF), D_MODEL ** -0.5),
        'w_mlp2': nrm(ks[22], (DEPTH, D_FF, D_MODEL), D_FF ** -0.5),
        'final_g': 1.0 + nrm(ks[23], (D_MODEL,), 0.01),
    }


def reference(x_prompt, x_sample, c_prompt, c_sample, cache_k, cache_v, state_pool, page_table,
              w_ada, b_ada, norm1_g, norm2_g, w_in, pool_w, pool_scale,
              lambda_q1, lambda_k1, lambda_q2, lambda_k2, subln_g, w_out, w_mlp1, w_mlp2, final_g):
    n_seq, n_pages = page_table.shape
    pos_p = jnp.arange(SEQ, dtype=jnp.int32)
    pos_s = PAST_LEN + jnp.arange(DEC_SEQ, dtype=jnp.int32)
    xp, xs = x_prompt, x_sample
    kp_l, vp_l, pp_l, ks_l, vs_l, ps_l = [], [], [], [], [], []
    for l in range(DEPTH):
        lam_init = 0.8 - 0.6 * math.exp(-0.3 * l)
        lam = (jnp.exp(jnp.sum(lambda_q1[l].astype(jnp.float32) * lambda_k1[l].astype(jnp.float32)))
               - jnp.exp(jnp.sum(lambda_q2[l].astype(jnp.float32) * lambda_k2[l].astype(jnp.float32)))
               + lam_init)

        sh1, sc1, g1, sh2, sc2, g2 = modulation(c_prompt, w_ada[l], b_ada[l])
        u_pool, q, k, v = mixer_inputs(xp, sh1, sc1, norm1_g[l], w_in[l], pos_p)
        u_ext = jnp.concatenate([jnp.zeros((xp.shape[0], POOL_BUF, POOL_WIDTH), u_pool.dtype), u_pool], axis=1)
        y_pool = pool_mix(u_ext, 0, pool_w[l], pool_scale[l])
        o = head_norm(attend_prompt(q, k, v, lam), subln_g[l], lam_init, xp.dtype)
        kp_l.append(k)
        vp_l.append(v)
        pp_l.append(u_ext[:, -POOL_BUF:])
        xp = mixer_output(xp, y_pool, o, g1, sh2, sc2, g2, w_out[l], norm2_g[l], w_mlp1[l], w_mlp2[l])

        sh1, sc1, g1, sh2, sc2, g2 = modulation(c_sample, w_ada[l], b_ada[l])
        u_pool, q, k, v = mixer_inputs(xs, sh1, sc1, norm1_g[l], w_in[l], pos_s)
        u_ext = jnp.concatenate([state_pool[l].astype(u_pool.dtype), u_pool], axis=1)
        y_pool = pool_mix(u_ext, PAST_LEN, pool_w[l], pool_scale[l])
        k_past = cache_k[l][page_table].reshape(n_seq, n_pages * PAGE_SIZE, 2 * N_HEADS, HEAD_DIM)
        v_past = cache_v[l][page_table].reshape(n_seq, n_pages * PAGE_SIZE, N_HEADS, 2 * HEAD_DIM)
        o = head_norm(attend_sample(q, k, v, k_past, v_past, lam), subln_g[l], lam_init, xs.dtype)
        ks_l.append(k)
        vs_l.append(v)
        ps_l.append(u_ext[:, -POOL_BUF:])
        xs = mixer_output(xs, y_pool, o, g1, sh2, sc2, g2, w_out[l], norm2_g[l], w_mlp1[l], w_mlp2[l])

    y_prompt = rmsnorm(xp, final_g)
    y_sample = rmsnorm(xs, final_g)
    return (y_prompt, y_sample, jnp.stack(kp_l), jnp.stack(vp_l), jnp.stack(pp_l),
            jnp.stack(ks_l), jnp.stack(vs_l), jnp.stack(ps_l))
```

```python
import functools
import math

import jax
import jax.numpy as jnp
from jax import lax
from jax.experimental import pallas as pl
from jax.experimental.pallas import tpu as pltpu

N_HEADS = 4
HEAD_DIM = 64
HALF = HEAD_DIM // 2
HEAD_W = 2 * HEAD_DIM
POOL_WINDOWS = (2, 4, 8, 16)
POOL_BUF = max(POOL_WINDOWS) - 1
ROPE_THETA = 10000.0
EPS = 1e-6
NEG_INF = -1e30
SCALE = HEAD_DIM ** -0.5
N_MOD = 6

LANES = 128
VMEM_LIMIT_BYTES = 56 * 1024 * 1024

POOL_PRE = 16
TOKEN_TILE = 512
ATTN_TILE = 512
PAGES_PER_CHUNK = 8
RING_SLOTS = 4
FF_CHUNK = 1024

F32 = jnp.float32
BF16 = jnp.bfloat16
NT_DIMS = (((1,), (1,)), ((), ()))


def _rms(x, g):
    return x * lax.rsqrt(jnp.mean(x * x, axis=-1, keepdims=True) + EPS) * g


def _params(n_grid):
    return pltpu.CompilerParams(dimension_semantics=("arbitrary",) * n_grid,
                                vmem_limit_bytes=VMEM_LIMIT_BYTES)


def _resident(shape):
    nd = len(shape)
    return pl.BlockSpec(shape, lambda *_: (0,) * nd, pipeline_mode=pl.Buffered(1))


def _mod_kernel(c_ref, w_ref, b_ref, o_ref):
    c = c_ref[...]
    s = c * jax.nn.sigmoid(c)
    o_ref[...] = jnp.dot(s.astype(BF16), w_ref[...].astype(BF16), preferred_element_type=F32) + b_ref[...]


def _modulation(c, w_ada, b_ada):
    n, d = c.shape
    return pl.pallas_call(
        _mod_kernel,
        grid=(N_MOD,),
        in_specs=[pl.BlockSpec((n, d), lambda j: (0, 0)),
                  pl.BlockSpec((d, d), lambda j: (0, j)),
                  pl.BlockSpec((1, d), lambda j: (0, j))],
        out_specs=pl.BlockSpec((n, d), lambda j: (0, j)),
        out_shape=jax.ShapeDtypeStruct((n, N_MOD * d), F32),
        compiler_params=_params(1),
        name="modulation",
    )(c, w_ada, b_ada.reshape(1, -1))


def _rope_rows(t, cos, sin_signed, first_half):
    outs = []
    for c in range(t.shape[1] // LANES):
        tc = t[:, c * LANES:(c + 1) * LANES]
        rot = jnp.where(first_half, pltpu.roll(tc, LANES - HALF, 1), pltpu.roll(tc, HALF, 1))
        outs.append(tc * cos + rot * sin_signed)
    return jnp.concatenate(outs, axis=1)


def _rope_cols(t, cos, sin):
    outs = []
    for c in range(t.shape[0] // HEAD_DIM):
        x1 = t[c * HEAD_DIM:c * HEAD_DIM + HALF, :]
        x2 = t[c * HEAD_DIM + HALF:(c + 1) * HEAD_DIM, :]
        outs += [x1 * cos - x2 * sin, x2 * cos + x1 * sin]
    return jnp.concatenate(outs, axis=0)


def _inproj_kernel(*refs, groups, rows, pos0, carried, attn_width, pool_width):
    if carried:
        (x_ref, shift_ref, scale_ref, g_ref, w_ref, wk_ref, cos_ref, sin_ref, pw_ref, ps_ref, cost_ref, sint_ref,
         k_ref, v_ref, q_ref, yp_ref, tail_ref, kb_ref, vb_ref, pre_ref) = refs
    else:
        (x_ref, shift_ref, scale_ref, g_ref, w_ref, wk_ref, cos_ref, sin_ref, pw_ref, ps_ref, pre_ref,
         k_ref, v_ref, q_ref, yp_ref, tail_ref) = refs
    G, R = groups, rows
    step = pl.program_id(1)

    if carried:
        @pl.when(step == 0)
        def _():
            pre_ref[...] = jnp.zeros_like(pre_ref)

    x = x_ref[...]
    h = _rms(x, g_ref[...]) * (1.0 + scale_ref[...]) + shift_ref[...]
    h = h.reshape(G * R, -1).astype(BF16)
    u = jnp.dot(h, w_ref[...], preferred_element_type=F32)
    u_pool = u[:, :pool_width]
    q = u[:, pool_width:pool_width + attn_width]
    v = u[:, pool_width + attn_width:]

    lane = lax.broadcasted_iota(jnp.int32, (G * R, LANES), 1)
    first_half = (lane % HEAD_DIM) < HALF
    q = _rope_rows(q, cos_ref[...], sin_ref[...], first_half)
    q_ref[...] = (q * SCALE).reshape(q_ref.shape).astype(q_ref.dtype)
    v_ref[...] = v.reshape(v_ref.shape)
    if carried:
        kt = lax.dot_general(wk_ref[...], h, NT_DIMS, preferred_element_type=F32)
        kt = _rope_cols(kt, cost_ref[...], sint_ref[...])
        k_ref[...] = kt.reshape(k_ref.shape)
        kb_ref[...] = kt.reshape(kb_ref.shape).astype(kb_ref.dtype)
        vb_ref[...] = v.reshape(vb_ref.shape).astype(vb_ref.dtype)
    else:
        k = lax.dot_general(h, wk_ref[...], NT_DIMS, preferred_element_type=F32)
        k_ref[...] = _rope_rows(k, cos_ref[...], sin_ref[...], first_half).reshape(k_ref.shape)

    u3 = u_pool.reshape(G, R, pool_width)
    ext3 = jnp.concatenate([pre_ref[...], u3], axis=1)
    tail = ext3[:, R:R + POOL_PRE, :]
    tail_ref[...] = tail
    ext = ext3.reshape(G * (POOL_PRE + R), pool_width)
    pos = pos0 + step * R + lax.broadcasted_iota(jnp.int32, (G, R, LANES), 1)
    ds = []
    for gi, w in enumerate(POOL_WINDOWS):
        a = ext[:, gi * LANES:(gi + 1) * LANES]
        sft = 1
        while sft < w:
            a = a + pltpu.roll(a, sft, 0)
            sft *= 2
        win = a.reshape(G, POOL_PRE + R, LANES)[:, POOL_PRE:, :]
        cnt = jnp.minimum(w, pos + 1).astype(F32)
        d = win / cnt - u3[:, :, gi * LANES:(gi + 1) * LANES]
        ds.append(d.reshape(G * R, LANES))
    dcat = jnp.concatenate(ds, axis=1)
    y = jnp.dot(dcat.astype(BF16), pw_ref[...], preferred_element_type=F32) * ps_ref[...]
    yp_ref[...] = y.reshape(yp_ref.shape).astype(yp_ref.dtype)
    if carried:
        pre_ref[...] = tail


def _inproj(x, shift_scale, g, w_pqv, wk_t, rope_rows, pool_bd, pool_scale, *, pos0, rope_cols=None, pre=None):
    B, L, D = x.shape
    pool_width = pool_bd.shape[0]
    attn_width = wk_t.shape[0]
    carried = pre is None
    if carried:
        G, R = 1, min(L, TOKEN_TILE)
        grid = (B, L // R)
    else:
        G, R = B, L
        grid = (1, 1)
    blk = lambda w: pl.BlockSpec((G, R, w), lambda b, l: (b, l, 0))
    kt_blk = pl.BlockSpec((G, attn_width, R), lambda b, l: (b, 0, l))
    mod_blk = lambda j: pl.BlockSpec((G, 1, D), lambda b, l: (b, 0, j))
    tab_blk = pl.BlockSpec((G * R, LANES), lambda b, l: (l, 0))
    pre_blk = pl.BlockSpec((G, POOL_PRE, pool_width), lambda b, l: (b, 0, 0))
    in_specs = [blk(D), mod_blk(0), mod_blk(1), _resident((1, D)), _resident(w_pqv.shape), _resident(wk_t.shape),
                tab_blk, tab_blk, _resident(pool_bd.shape), _resident((1, pool_width))]
    args = [x, shift_scale, shift_scale, g.reshape(1, D), w_pqv, wk_t, *rope_rows, pool_bd,
            pool_scale.reshape(1, pool_width)]
    act = BF16 if carried else F32
    out_shape = [jax.ShapeDtypeStruct((B, attn_width, L) if carried else (B, L, attn_width), F32),
                 jax.ShapeDtypeStruct((B, L, attn_width), F32),
                 jax.ShapeDtypeStruct((B, L, attn_width), act),
                 jax.ShapeDtypeStruct((B, L, pool_width), act),
                 jax.ShapeDtypeStruct((B, POOL_PRE, pool_width), F32)]
    out_specs = [kt_blk if carried else blk(attn_width), blk(attn_width), blk(attn_width), blk(pool_width), pre_blk]
    scratch = []
    if carried:
        colt_blk = pl.BlockSpec((HALF, R), lambda b, l: (0, l))
        in_specs += [colt_blk, colt_blk]
        args += list(rope_cols)
        out_shape += [jax.ShapeDtypeStruct((B, attn_width, L), BF16),
                      jax.ShapeDtypeStruct((B, L, attn_width), BF16)]
        out_specs += [kt_blk, blk(attn_width)]
        scratch.append(pltpu.VMEM((G, POOL_PRE, pool_width), F32))
    else:
        in_specs.append(pre_blk)
        args.append(pre)
    kern = functools.partial(_inproj_kernel, groups=G, rows=R, pos0=pos0, carried=carried,
                             attn_width=attn_width, pool_width=pool_width)
    return pl.pallas_call(kern, grid=grid, in_specs=in_specs, out_specs=out_specs, out_shape=out_shape,
                          scratch_shapes=scratch, compiler_params=_params(2), name="inproj")(*args)


def _lambda(lq1_ref, lk1_ref, lq2_ref, lk2_ref, lam_init):
    return (jnp.exp(jnp.sum(lq1_ref[...] * lk1_ref[...], axis=1, keepdims=True))
            - jnp.exp(jnp.sum(lq2_ref[...] * lk2_ref[...], axis=1, keepdims=True)) + lam_init)


def _head_norm(o, g, lam_init):
    return _rms(o, g) * (1.0 - lam_init)


def _prompt_attn_kernel(qi_ref, ki_ref, q_ref, kt_ref, v_ref, lq1_ref, lk1_ref, lq2_ref, lk2_ref, g_ref,
                        o_ref, qs_ref, m_ref, l_ref, acc_ref, *, tile, lam_init):
    t = pl.program_id(2)
    qi = qi_ref[t]
    ki = ki_ref[t]

    @pl.when(ki == 0)
    def _():
        q = q_ref[...]
        lane = lax.broadcasted_iota(jnp.int32, q.shape, 1)
        qs_ref[:tile, :] = jnp.where(lane < HEAD_DIM, q, jnp.zeros_like(q))
        qs_ref[tile:, :] = jnp.where(lane >= HEAD_DIM, q, jnp.zeros_like(q))
        m_ref[...] = jnp.full_like(m_ref, NEG_INF)
        l_ref[...] = jnp.zeros_like(l_ref)
        acc_ref[...] = jnp.zeros_like(acc_ref)

    def step(masked):
        s = jnp.dot(qs_ref[...], kt_ref[...], preferred_element_type=F32)
        if masked:
            row = lax.broadcasted_iota(jnp.int32, s.shape, 0) % tile
            col = lax.broadcasted_iota(jnp.int32, s.shape, 1)
            s = jnp.where(col <= row, s, NEG_INF)
        m_prev = m_ref[...]
        m_new = jnp.maximum(m_prev, jnp.max(s, axis=1, keepdims=True))
        alpha = jnp.exp(m_prev - m_new)
        p = jnp.exp(s - m_new)
        l_ref[...] = alpha * l_ref[...] + jnp.sum(p, axis=1, keepdims=True)
        acc_ref[...] = alpha * acc_ref[...] + jnp.dot(p.astype(BF16), v_ref[...], preferred_element_type=F32)
        m_ref[...] = m_new

    @pl.when(ki < qi)
    def _():
        step(False)

    @pl.when(ki == qi)
    def _():
        step(True)
        lam = _lambda(lq1_ref, lk1_ref, lq2_ref, lk2_ref, lam_init)
        on = acc_ref[...] / l_ref[...]
        o = on[:tile, :] - lam * on[tile:, :]
        o_ref[...] = _head_norm(o, g_ref[...], lam_init).astype(o_ref.dtype)


def _prompt_attention(q, kt, v, lams, subln_g, lam_init):
    B, S, W = q.shape
    tile = min(S, ATTN_TILE)
    nq = S // tile
    qi_tab = jnp.asarray([qi for qi in range(nq) for _ in range(qi + 1)], jnp.int32)
    ki_tab = jnp.asarray([ki for qi in range(nq) for ki in range(qi + 1)], jnp.int32)
    lam_blk = pl.BlockSpec((1, HEAD_DIM), lambda b, h, t, qt, kt_: (0, 0))
    grid_spec = pltpu.PrefetchScalarGridSpec(
        num_scalar_prefetch=2,
        grid=(B, W // HEAD_W, int(qi_tab.shape[0])),
        in_specs=[pl.BlockSpec((None, tile, HEAD_W), lambda b, h, t, qt, kt_: (b, qt[t], h)),
                  pl.BlockSpec((None, HEAD_W, tile), lambda b, h, t, qt, kt_: (b, h, kt_[t])),
                  pl.BlockSpec((None, tile, HEAD_W), lambda b, h, t, qt, kt_: (b, kt_[t], h)),
                  lam_blk, lam_blk, lam_blk, lam_blk,
                  pl.BlockSpec((1, HEAD_W), lambda b, h, t, qt, kt_: (0, 0))],
        out_specs=pl.BlockSpec((None, tile, HEAD_W), lambda b, h, t, qt, kt_: (b, qt[t], h)),
        scratch_shapes=[pltpu.VMEM((2 * tile, HEAD_W), BF16),
                        pltpu.VMEM((2 * tile, 1), F32),
                        pltpu.VMEM((2 * tile, 1), F32),
                        pltpu.VMEM((2 * tile, HEAD_W), F32)])
    kern = functools.partial(_prompt_attn_kernel, tile=tile, lam_init=lam_init)
    return pl.pallas_call(kern, grid_spec=grid_spec, out_shape=jax.ShapeDtypeStruct((B, S, W), BF16),
                          compiler_params=_params(3), name="prompt_attention")(
        qi_tab, ki_tab, q, kt, v, *[x.reshape(1, HEAD_DIM) for x in lams], subln_g.reshape(1, HEAD_W))


def _key_page_copy(ck_hbm, page, buf_ref, slot, j, sem):
    n = ck_hbm.shape[2]
    return pltpu.make_async_copy(ck_hbm.at[page], buf_ref.at[slot, :, pl.ds(j * n, n)], sem.at[slot])


def _value_page_copy(cv_hbm, page, buf_ref, slot, j, sem):
    n = cv_hbm.shape[1]
    return pltpu.make_async_copy(cv_hbm.at[page], buf_ref.at[slot, pl.ds(j * n, n), :], sem.at[slot])


def _sample_attn_kernel(pt_ref, q_ref, kn_ref, vn_ref, lq1_ref, lk1_ref, lq2_ref, lk2_ref, g_ref, ck_hbm, cv_hbm,
                        o_ref, kbuf_ref, vbuf_ref, s_ref, p_ref, ksem, vsem, *, n_pages, lam_init):
    b = pl.program_id(0)
    n_seq = pl.num_programs(0)
    page_rows = ck_hbm.shape[2]
    chunk = PAGES_PER_CHUNK * page_rows
    n_chunks = n_pages // PAGES_PER_CHUNK
    units = 2 * n_chunks
    past = n_pages * page_rows
    L, W = q_ref.shape
    rows_per_head = 2 * L

    def unit_copies(seq, i, slot):
        c = i % n_chunks
        pages = [pt_ref[seq * n_pages + c * PAGES_PER_CHUNK + j] for j in range(PAGES_PER_CHUNK)]
        if i < n_chunks:
            return [_key_page_copy(ck_hbm, pg, kbuf_ref, slot, j, ksem) for j, pg in enumerate(pages)]
        return [_value_page_copy(cv_hbm, pg, vbuf_ref, slot, j, vsem) for j, pg in enumerate(pages)]

    def start_unit(seq, i):
        for cp in unit_copies(seq, i, i % RING_SLOTS):
            cp.start()

    @pl.when(b == 0)
    def _():
        for i in range(RING_SLOTS - 1):
            start_unit(b, i)

    q = q_ref[...]
    q2 = jnp.concatenate([q] * (2 * N_HEADS), axis=0)
    r_id = lax.broadcasted_iota(jnp.int32, q2.shape, 0) // L
    c_id = lax.broadcasted_iota(jnp.int32, q2.shape, 1) // HEAD_DIM
    q2 = jnp.where(r_id == c_id, q2, 0.0).astype(BF16)
    pad = jnp.zeros((LANES - L, W), F32)
    kn = jnp.concatenate([kn_ref[...], pad], axis=0).astype(BF16)
    vn = jnp.concatenate([vn_ref[...], pad], axis=0).astype(BF16)

    acc = [jnp.zeros((rows_per_head, HEAD_W), F32) for _ in range(N_HEADS)]
    for i in range(units):
        slot = i % RING_SLOTS
        nxt = i + RING_SLOTS - 1
        if nxt < units:
            start_unit(b, nxt)
        else:
            @pl.when(b + 1 < n_seq)
            def _():
                start_unit(b + 1, nxt - units)
        for cp in unit_copies(b, i, slot):
            cp.wait()
        if i < n_chunks:
            kt = kbuf_ref[slot].astype(BF16)
            s_ref[:, i * chunk:(i + 1) * chunk] = jnp.dot(q2, kt, preferred_element_type=F32)
            if i == n_chunks - 1:
                sn = lax.dot_general(q2, kn, NT_DIMS, preferred_element_type=F32)
                key = lax.broadcasted_iota(jnp.int32, sn.shape, 1)
                qry = lax.broadcasted_iota(jnp.int32, sn.shape, 0) % L
                s_ref[:, past:] = jnp.where(key <= qry, sn, NEG_INF)
                s = s_ref[...]
                m = jnp.max(s, axis=1, keepdims=True)
                p = jnp.exp(s - m)
                l = jnp.sum(p, axis=1, keepdims=True)
                p_ref[...] = p.astype(BF16)
        else:
            c = i - n_chunks
            for h in range(N_HEADS):
                vh = vbuf_ref[slot, pl.ds(h, chunk, stride=N_HEADS), :].astype(BF16)
                ph = p_ref[h * rows_per_head:(h + 1) * rows_per_head, c * chunk:(c + 1) * chunk]
                acc[h] = acc[h] + jnp.dot(ph, vh, preferred_element_type=F32)
    lam = _lambda(lq1_ref, lk1_ref, lq2_ref, lk2_ref, lam_init)
    outs = []
    for h in range(N_HEADS):
        rows = slice(h * rows_per_head, (h + 1) * rows_per_head)
        a = acc[h] + jnp.dot(p_ref[rows, past:], vn[:, h * HEAD_W:(h + 1) * HEAD_W], preferred_element_type=F32)
        on = a / l[rows, :]
        outs.append(_head_norm(on[:L, :] - lam * on[L:, :], g_ref[...], lam_init))
    o_ref[...] = jnp.concatenate(outs, axis=1)


def _sample_attention(q, k_new, v_new, ck, cv, page_table, lams, subln_g, lam_init):
    B, L, W = q.shape
    n_pages = page_table.shape[1]
    page_rows = ck.shape[2]
    chunk = PAGES_PER_CHUNK * page_rows
    past = n_pages * page_rows
    n_rows = 2 * N_HEADS * L
    new_blk = pl.BlockSpec((None, L, W), lambda b, pt: (b, 0, 0))
    lam_blk = pl.BlockSpec((1, HEAD_DIM), lambda b, pt: (0, 0))
    grid_spec = pltpu.PrefetchScalarGridSpec(
        num_scalar_prefetch=1,
        grid=(B,),
        in_specs=[new_blk, new_blk, new_blk, lam_blk, lam_blk, lam_blk, lam_blk,
                  pl.BlockSpec((1, HEAD_W), lambda b, pt: (0, 0)),
                  pl.BlockSpec(memory_space=pl.ANY), pl.BlockSpec(memory_space=pl.ANY)],
        out_specs=new_blk,
        scratch_shapes=[pltpu.VMEM((RING_SLOTS, W, chunk), ck.dtype),
                        pltpu.VMEM((RING_SLOTS, chunk * N_HEADS, HEAD_W), cv.dtype),
                        pltpu.VMEM((n_rows, past + LANES), F32),
                        pltpu.VMEM((n_rows, past + LANES), BF16),
                        pltpu.SemaphoreType.DMA((RING_SLOTS,)),
                        pltpu.SemaphoreType.DMA((RING_SLOTS,))])
    kern = functools.partial(_sample_attn_kernel, n_pages=n_pages, lam_init=lam_init)
    return pl.pallas_call(kern, grid_spec=grid_spec, out_shape=jax.ShapeDtypeStruct((B, L, W), F32),
                          compiler_params=_params(1), name="sample_attention")(
        page_table.reshape(-1), q, k_new, v_new, *[x.reshape(1, HEAD_DIM) for x in lams],
        subln_g.reshape(1, HEAD_W), ck, cv)


def _mlp_kernel(x_ref, yp_ref, o_ref, g1_ref, sh2_ref, sc2_ref, g2_ref, n2_ref, nf_ref, wo_ref, w1_ref, w2_ref,
                y_ref, *, final_norm):
    G, R, D = x_ref.shape
    x = x_ref[...]
    cat = jnp.concatenate([yp_ref[...].reshape(G * R, -1).astype(BF16), o_ref[...].reshape(G * R, -1).astype(BF16)],
                          axis=1)
    mix = jnp.dot(cat, wo_ref[...], preferred_element_type=F32)
    x = x + g1_ref[...] * mix.reshape(G, R, D)
    h = (_rms(x, n2_ref[...]) * (1.0 + sc2_ref[...]) + sh2_ref[...]).reshape(G * R, D).astype(BF16)
    acc = jnp.zeros((G * R, D), F32)
    for c in range(w1_ref.shape[1] // FF_CHUNK):
        f = jnp.dot(h, w1_ref[:, c * FF_CHUNK:(c + 1) * FF_CHUNK], preferred_element_type=F32)
        f = jnp.square(jnp.maximum(f, 0.0)).astype(BF16)
        acc = acc + jnp.dot(f, w2_ref[c * FF_CHUNK:(c + 1) * FF_CHUNK, :], preferred_element_type=F32)
    x = x + g2_ref[...] * acc.reshape(G, R, D)
    if final_norm:
        x = _rms(x, nf_ref[...])
    y_ref[...] = x


def _mlp(x, y_pool, o, mod, norm2_g, final_g, w_out, w1, w2, *, group_rows, final_norm):
    B, L, D = x.shape
    if L >= group_rows:
        G, R = 1, group_rows
    else:
        G, R = group_rows // L, L
    grid = (B // G, L // R)
    blk = lambda w: pl.BlockSpec((G, R, w), lambda b, l: (b, l, 0))
    mod_blk = lambda j: pl.BlockSpec((G, 1, D), lambda b, l: (b, 0, j))
    kern = functools.partial(_mlp_kernel, final_norm=final_norm)
    return pl.pallas_call(
        kern, grid=grid,
        in_specs=[blk(D), blk(y_pool.shape[2]), blk(o.shape[2]), mod_blk(2), mod_blk(3), mod_blk(4), mod_blk(5),
                  _resident((1, D)), _resident((1, D)), _resident(w_out.shape), _resident(w1.shape),
                  _resident(w2.shape)],
        out_specs=blk(D),
        out_shape=jax.ShapeDtypeStruct((B, L, D), F32),
        compiler_params=_params(2), name="mlp")(
        x, y_pool, o, mod, mod, mod, mod, norm2_g.reshape(1, D), final_g.reshape(1, D), w_out, w1, w2)


def _rope_angles(pos):
    inv_freq = ROPE_THETA ** (-jnp.arange(HALF, dtype=F32) * 2.0 / HEAD_DIM)
    ang = pos.astype(F32)[:, None] * inv_freq[None, :]
    return jnp.cos(ang), jnp.sin(ang)


def _rope_row_tables(cos, sin):
    reps = LANES // HEAD_DIM
    return jnp.tile(cos, (1, 2 * reps)), jnp.tile(jnp.concatenate([-sin, sin], axis=1), (1, reps))


def _block_diag(w):
    g, c, d = w.shape
    out = jnp.zeros((g * c, g * d), w.dtype)
    for i in range(g):
        out = out.at[i * c:(i + 1) * c, i * d:(i + 1) * d].set(w[i])
    return out


def kernel(x_prompt, x_sample, c_prompt, c_sample, cache_k, cache_v, state_pool, page_table, w_ada, b_ada, norm1_g, norm2_g, w_in, pool_w, pool_scale, lambda_q1, lambda_k1, lambda_q2, lambda_k2, subln_g, w_out, w_mlp1, w_mlp2, final_g):
    depth = w_ada.shape[0]
    B, S, D = x_prompt.shape
    NB, L, _ = x_sample.shape
    n_pages = page_table.shape[1]
    n_pool_pages, page_rows = cache_k.shape[1], cache_k.shape[2]
    past_len = n_pages * page_rows
    attn_w = cache_k.shape[3] * cache_k.shape[4]
    pool_w_cols = pool_scale.shape[1]

    cos_p, sin_p = _rope_angles(jnp.arange(S, dtype=jnp.int32))
    cos_s, sin_s = _rope_angles(past_len + jnp.arange(L, dtype=jnp.int32))
    rows_p = _rope_row_tables(cos_p, sin_p)
    cols_p = (cos_p.T, sin_p.T)
    rows_s = tuple(jnp.tile(t, (NB, 1)) for t in _rope_row_tables(cos_s, sin_s))
    c_all = jnp.concatenate([c_prompt, c_sample], axis=0)

    xp, xs = x_prompt, x_sample
    kp_l, vp_l, pp_l, ks_l, vs_l, ps_l = [], [], [], [], [], []
    for l in range(depth):
        lam_init = 0.8 - 0.6 * math.exp(-0.3 * l)
        lams = (lambda_q1[l], lambda_k1[l], lambda_q2[l], lambda_k2[l])
        k_lo, k_hi = pool_w_cols + attn_w, pool_w_cols + 2 * attn_w
        w_pqv = jnp.concatenate([w_in[l][:, :k_lo], w_in[l][:, k_hi:]], axis=1).astype(BF16)
        wk_t = w_in[l][:, k_lo:k_hi].T.astype(BF16)
        w_out_b, w1_b, w2_b = w_out[l].astype(BF16), w_mlp1[l].astype(BF16), w_mlp2[l].astype(BF16)
        pool_bd = _block_diag(pool_w[l]).astype(BF16)
        last = l == depth - 1

        mod = _modulation(c_all, w_ada[l], b_ada[l])
        mod_p = mod[:B].reshape(B, 1, N_MOD * D)
        mod_s = mod[B:].reshape(NB, 1, N_MOD * D)

        kt, v, qb, yp, tail, ktb, vb = _inproj(xp, mod_p, norm1_g[l], w_pqv, wk_t, rows_p, pool_bd, pool_scale[l],
                                               pos0=0, rope_cols=cols_p)
        o = _prompt_attention(qb, ktb, vb, lams, subln_g[l], lam_init)
        kp_l.append(kt.reshape(B, 2 * N_HEADS, HEAD_DIM, S).transpose(0, 3, 1, 2))
        vp_l.append(v.reshape(B, S, N_HEADS, HEAD_W))
        pp_l.append(tail[:, POOL_PRE - POOL_BUF:])
        xp = _mlp(xp, yp, o, mod_p, norm2_g[l], final_g, w_out_b, w1_b, w2_b, group_rows=TOKEN_TILE, final_norm=last)

        pre = jnp.pad(state_pool[l], ((0, 0), (POOL_PRE - POOL_BUF, 0), (0, 0)))
        k, v, qf, yp, tail = _inproj(xs, mod_s, norm1_g[l], w_pqv, wk_t, rows_s, pool_bd, pool_scale[l],
                                     pos0=past_len, pre=pre)
        ck = cache_k[l].transpose(0, 2, 3, 1).reshape(n_pool_pages, attn_w, page_rows)
        cv = cache_v[l].reshape(n_pool_pages, page_rows * N_HEADS, HEAD_W)
        o = _sample_attention(qf, k, v, ck, cv, page_table, lams, subln_g[l], lam_init)
        ks_l.append(k.reshape(NB, L, 2 * N_HEADS, HEAD_DIM))
        vs_l.append(v.reshape(NB, L, N_HEADS, HEAD_W))
        ps_l.append(tail[:, POOL_PRE - POOL_BUF:])
        xs = _mlp(xs, yp, o, mod_s, norm2_g[l], final_g, w_out_b, w1_b, w2_b, group_rows=TOKEN_TILE, final_norm=last)

    return (xp, xs, jnp.stack(kp_l), jnp.stack(vp_l), jnp.stack(pp_l),
            jnp.stack(ks_l), jnp.stack(vs_l), jnp.stack(ps_l))
```

```python
import functools
import math

import jax
import jax.numpy as jnp
from jax import lax
from jax.experimental import pallas as pl
from jax.experimental.pallas import tpu as pltpu

N_HEADS = 4
HEAD_DIM = 64
HALF = HEAD_DIM // 2
HEAD_W = 2 * HEAD_DIM
POOL_WINDOWS = (2, 4, 8, 16)
POOL_BUF = max(POOL_WINDOWS) - 1
ROPE_THETA = 10000.0
EPS = 1e-6
NEG_INF = -1e30
SCALE = HEAD_DIM ** -0.5
Q_SCALE = SCALE * math.log2(math.e)
N_MOD = 6

LANES = 128
VMEM_LIMIT_BYTES = 56 * 1024 * 1024

POOL_PRE = 16
TOKEN_TILE = 512
ATTN_TILE = 512
ATTN_COLS = 256
PAGES_PER_CHUNK = 8
RING_SLOTS = 4
FF_CHUNK = 1024

F32 = jnp.float32
BF16 = jnp.bfloat16
NT_DIMS = (((1,), (1,)), ((), ()))


def _rms(x, g):
    return x * lax.rsqrt(jnp.mean(x * x, axis=-1, keepdims=True) + EPS) * g


def _params(n_grid):
    return pltpu.CompilerParams(dimension_semantics=("arbitrary",) * n_grid,
                                vmem_limit_bytes=VMEM_LIMIT_BYTES)


def _resident(shape):
    nd = len(shape)
    return pl.BlockSpec(shape, lambda *_: (0,) * nd, pipeline_mode=pl.Buffered(1))


def _mod_kernel(c_ref, w_ref, b_ref, o_ref):
    c = c_ref[...]
    s = c * jax.nn.sigmoid(c)
    o_ref[...] = jnp.dot(s.astype(BF16), w_ref[...].astype(BF16), preferred_element_type=F32) + b_ref[...]


def _modulation(c, w_ada, b_ada):
    n, d = c.shape
    return pl.pallas_call(
        _mod_kernel,
        grid=(N_MOD,),
        in_specs=[pl.BlockSpec((n, d), lambda j: (0, 0)),
                  pl.BlockSpec((d, d), lambda j: (0, j)),
                  pl.BlockSpec((1, d), lambda j: (0, j))],
        out_specs=pl.BlockSpec((n, d), lambda j: (0, j)),
        out_shape=jax.ShapeDtypeStruct((n, N_MOD * d), F32),
        compiler_params=_params(1),
        name="modulation",
    )(c, w_ada, b_ada.reshape(1, -1))


def _rope_rows(t, cos, sin_signed, first_half):
    outs = []
    for c in range(t.shape[1] // LANES):
        tc = t[:, c * LANES:(c + 1) * LANES]
        rot = jnp.where(first_half, pltpu.roll(tc, LANES - HALF, 1), pltpu.roll(tc, HALF, 1))
        outs.append(tc * cos + rot * sin_signed)
    return jnp.concatenate(outs, axis=1)


def _rope_cols(t, cos, sin):
    outs = []
    for c in range(t.shape[0] // HEAD_DIM):
        x1 = t[c * HEAD_DIM:c * HEAD_DIM + HALF, :]
        x2 = t[c * HEAD_DIM + HALF:(c + 1) * HEAD_DIM, :]
        outs += [x1 * cos - x2 * sin, x2 * cos + x1 * sin]
    return jnp.concatenate(outs, axis=0)


def _pool_mix(u_pool, pre, pw_ref, ps_ref, pos_base, G, R):
    C = u_pool.shape[1]
    u3 = u_pool.reshape(G, R, C)
    ext3 = jnp.concatenate([pre, u3], axis=1)
    tail = ext3[:, R:R + POOL_PRE, :]
    ext = ext3.reshape(G * (POOL_PRE + R), C)
    pos = pos_base + lax.broadcasted_iota(jnp.int32, (G, R, LANES), 1)
    ds = []
    for gi, w in enumerate(POOL_WINDOWS):
        a = ext[:, gi * LANES:(gi + 1) * LANES]
        sft = 1
        while sft < w:
            a = a + pltpu.roll(a, sft, 0)
            sft *= 2
        win = a.reshape(G, POOL_PRE + R, LANES)[:, POOL_PRE:, :]
        cnt = jnp.minimum(w, pos + 1).astype(F32)
        d = win / cnt - u3[:, :, gi * LANES:(gi + 1) * LANES]
        ds.append(d.reshape(G * R, LANES))
    dcat = jnp.concatenate(ds, axis=1)
    y = jnp.dot(dcat.astype(BF16), pw_ref[...], preferred_element_type=F32) * ps_ref[...]
    return y, tail


def _inproj_prompt_kernel(x_ref, shift_ref, scale_ref, g_ref, wp_ref, wqkv_ref, cos_ref, sin_ref, pw_ref, ps_ref,
                          kt_ref, v_ref, qtb_ref, kb_ref, vtb_ref, yp_ref, tail_ref, pre_ref, *, attn_width):
    _, R, _ = x_ref.shape
    A = attn_width
    step = pl.program_id(1)

    @pl.when(step == 0)
    def _():
        pre_ref[...] = jnp.zeros_like(pre_ref)

    h = _rms(x_ref[...], g_ref[...]) * (1.0 + scale_ref[...]) + shift_ref[...]
    h = h.reshape(R, -1).astype(BF16)
    ut = lax.dot_general(wqkv_ref[...], h, NT_DIMS, preferred_element_type=F32)
    cos, sin = cos_ref[...], sin_ref[...]
    qt = _rope_cols(ut[:A], cos, sin) * Q_SCALE
    kt = _rope_cols(ut[A:2 * A], cos, sin)
    vt = ut[2 * A:]
    qtb_ref[...] = qt.reshape(qtb_ref.shape).astype(qtb_ref.dtype)
    kt_ref[...] = kt.reshape(kt_ref.shape)
    vtb_ref[...] = vt.reshape(vtb_ref.shape).astype(vtb_ref.dtype)
    kb_ref[...] = kt.T.reshape(kb_ref.shape).astype(kb_ref.dtype)
    n_heads = A // HEAD_W
    for hd in range(n_heads):
        v_ref[0, pl.ds(hd, R, stride=n_heads), :] = vt[hd * HEAD_W:(hd + 1) * HEAD_W, :].T

    u_pool = jnp.dot(h, wp_ref[...], preferred_element_type=F32)
    y, tail = _pool_mix(u_pool, pre_ref[...], pw_ref, ps_ref, step * R, 1, R)
    yp_ref[...] = y.reshape(yp_ref.shape).astype(yp_ref.dtype)
    tail_ref[...] = tail
    pre_ref[...] = tail


def _inproj_sample_kernel(x_ref, shift_ref, scale_ref, g_ref, wp_ref, wqkv_ref, cos_ref, sin_ref, pw_ref, ps_ref,
                          pre_ref, k_ref, v_ref, q_ref, yp_ref, tail_ref, *, attn_width, pos0):
    G, R, _ = x_ref.shape
    A = attn_width
    h = _rms(x_ref[...], g_ref[...]) * (1.0 + scale_ref[...]) + shift_ref[...]
    h = h.reshape(G * R, -1).astype(BF16)
    u = lax.dot_general(h, wqkv_ref[...], NT_DIMS, preferred_element_type=F32)
    lane = lax.broadcasted_iota(jnp.int32, (G * R, LANES), 1)
    first_half = (lane % HEAD_DIM) < HALF
    cos, sin_signed = cos_ref[...], sin_ref[...]
    q_ref[...] = (_rope_rows(u[:, :A], cos, sin_signed, first_half) * Q_SCALE).reshape(q_ref.shape)
    k_ref[...] = _rope_rows(u[:, A:2 * A], cos, sin_signed, first_half).reshape(k_ref.shape)
    v_ref[...] = u[:, 2 * A:].reshape(v_ref.shape)

    u_pool = jnp.dot(h, wp_ref[...], preferred_element_type=F32)
    y, tail = _pool_mix(u_pool, pre_ref[...], pw_ref, ps_ref, pos0, G, R)
    yp_ref[...] = y.reshape(yp_ref.shape)
    tail_ref[...] = tail


def _inproj_prompt(x, shift_scale, g, w_pool, w_qkv_t, cos_t, sin_t, pool_bd, pool_scale):
    B, S, D = x.shape
    C = pool_bd.shape[0]
    A = w_qkv_t.shape[0] // 3
    R = min(S, TOKEN_TILE)
    rows = lambda w: pl.BlockSpec((1, R, w), lambda b, l: (b, l, 0))
    cols = pl.BlockSpec((1, A, R), lambda b, l: (b, 0, l))
    mod_blk = lambda j: pl.BlockSpec((1, 1, D), lambda b, l: (b, 0, j))
    tab_blk = pl.BlockSpec((HALF, R), lambda b, l: (0, l))
    tail_blk = pl.BlockSpec((1, POOL_PRE, C), lambda b, l: (b, 0, 0))
    n_heads = A // HEAD_W
    v_rows = pl.BlockSpec((1, R * n_heads, HEAD_W), lambda b, l: (b, l, 0))
    out_shape = [jax.ShapeDtypeStruct((B, A, S), F32), jax.ShapeDtypeStruct((B, S * n_heads, HEAD_W), F32),
                 jax.ShapeDtypeStruct((B, A, S), BF16), jax.ShapeDtypeStruct((B, S, A), BF16),
                 jax.ShapeDtypeStruct((B, A, S), BF16), jax.ShapeDtypeStruct((B, S, C), BF16),
                 jax.ShapeDtypeStruct((B, POOL_PRE, C), F32)]
    return pl.pallas_call(
        functools.partial(_inproj_prompt_kernel, attn_width=A),
        grid=(B, S // R),
        in_specs=[rows(D), mod_blk(0), mod_blk(1), _resident((1, D)), _resident(w_pool.shape),
                  _resident(w_qkv_t.shape), tab_blk, tab_blk, _resident(pool_bd.shape), _resident((1, C))],
        out_specs=[cols, v_rows, cols, rows(A), cols, rows(C), tail_blk],
        out_shape=out_shape,
        scratch_shapes=[pltpu.VMEM((1, POOL_PRE, C), F32)],
        compiler_params=_params(2), name="inproj_prompt")(
        x, shift_scale, shift_scale, g.reshape(1, D), w_pool, w_qkv_t, cos_t, sin_t, pool_bd,
        pool_scale.reshape(1, C))


def _inproj_sample(x, shift_scale, g, w_pool, w_qkv_t, cos_rows, sin_rows, pool_bd, pool_scale, pre, *, pos0):
    B, L, D = x.shape
    C = pool_bd.shape[0]
    A = w_qkv_t.shape[0] // 3
    full = lambda *shape: pl.BlockSpec(shape, lambda i: (0,) * len(shape))
    mod_blk = lambda j: pl.BlockSpec((B, 1, D), lambda i: (0, 0, j))
    out_shape = [jax.ShapeDtypeStruct((B, L, A), F32)] * 3 + [jax.ShapeDtypeStruct((B, L, C), F32),
                                                               jax.ShapeDtypeStruct((B, POOL_PRE, C), F32)]
    return pl.pallas_call(
        functools.partial(_inproj_sample_kernel, attn_width=A, pos0=pos0),
        grid=(1,),
        in_specs=[full(B, L, D), mod_blk(0), mod_blk(1), full(1, D), full(*w_pool.shape), full(*w_qkv_t.shape),
                  full(B * L, LANES), full(B * L, LANES), full(*pool_bd.shape), full(1, C), full(B, POOL_PRE, C)],
        out_specs=[full(B, L, A)] * 3 + [full(B, L, C), full(B, POOL_PRE, C)],
        out_shape=out_shape,
        compiler_params=_params(1), name="inproj_sample")(
        x, shift_scale, shift_scale, g.reshape(1, D), w_pool, w_qkv_t, cos_rows, sin_rows, pool_bd,
        pool_scale.reshape(1, C), pre)


def _lambda(lq1_ref, lk1_ref, lq2_ref, lk2_ref, lam_init):
    return (jnp.exp(jnp.sum(lq1_ref[...] * lk1_ref[...], axis=1, keepdims=True))
            - jnp.exp(jnp.sum(lq2_ref[...] * lk2_ref[...], axis=1, keepdims=True)) + lam_init)


def _head_norm(o, g, lam_init):
    return _rms(o, g) * (1.0 - lam_init)


def _prompt_attn_kernel(qi_ref, ki_ref, qt_ref, k_ref, vt_ref, lq1_ref, lk1_ref, lq2_ref, lk2_ref, g_ref,
                        o_ref, qs_ref, m_ref, l_ref, acc_ref, s_ref, *, tile, lam_init):
    t = pl.program_id(1)
    qi = qi_ref[t]
    ki = ki_ref[t]
    n_heads = qs_ref.shape[0]

    @pl.when(ki == 0)
    def _():
        zeros = jnp.zeros((HEAD_DIM, tile), qs_ref.dtype)
        for h in range(n_heads):
            qs_ref[h, :HEAD_DIM, :tile] = qt_ref[h * HEAD_W:h * HEAD_W + HEAD_DIM, :]
            qs_ref[h, HEAD_DIM:, :tile] = zeros
            qs_ref[h, :HEAD_DIM, tile:] = zeros
            qs_ref[h, HEAD_DIM:, tile:] = qt_ref[h * HEAD_W + HEAD_DIM:(h + 1) * HEAD_W, :]
        m_ref[...] = jnp.full_like(m_ref, NEG_INF)
        l_ref[...] = jnp.zeros_like(l_ref)
        acc_ref[...] = jnp.zeros_like(acc_ref)

    def scores(h, masked):
        s = jnp.dot(k_ref[:, h * HEAD_W:(h + 1) * HEAD_W], qs_ref[h], preferred_element_type=F32)
        if masked:
            key = lax.broadcasted_iota(jnp.int32, s.shape, 0)
            qry = lax.broadcasted_iota(jnp.int32, s.shape, 1) % tile
            s = jnp.where(key <= qry, s, NEG_INF)
        s_ref[h % 2] = s
        return jnp.max(s, axis=0, keepdims=True)

    def accumulate(h, m_cur):
        m_prev = m_ref[h]
        m_new = jnp.maximum(m_prev, m_cur)
        alpha = jnp.exp2(m_prev - m_new)
        p = jnp.exp2(s_ref[h % 2] - m_new)
        l_ref[h] = alpha * l_ref[h] + jnp.sum(p, axis=0, keepdims=True)
        acc_ref[h] = alpha * acc_ref[h] + jnp.dot(vt_ref[h * HEAD_W:(h + 1) * HEAD_W, :], p.astype(BF16),
                                                  preferred_element_type=F32)
        m_ref[h] = m_new

    def step(masked):
        m_cur = [None] * n_heads
        for h in range(n_heads):
            m_cur[h] = scores(h, masked)
            if h > 0:
                accumulate(h - 1, m_cur[h - 1])
        accumulate(n_heads - 1, m_cur[n_heads - 1])

    @pl.when(ki < qi)
    def _():
        step(False)

    @pl.when(ki == qi)
    def _():
        step(True)
        lam = _lambda(lq1_ref, lk1_ref, lq2_ref, lk2_ref, lam_init)
        for h in range(n_heads):
            on = acc_ref[h] / l_ref[h]
            o = on[:, :tile] - lam * on[:, tile:]
            y = o * lax.rsqrt(jnp.mean(o * o, axis=0, keepdims=True) + EPS) * g_ref[...] * (1.0 - lam_init)
            o_ref[:, h * HEAD_W:(h + 1) * HEAD_W] = y.T.astype(o_ref.dtype)


def _prompt_attention(qt, k, vt, lams, subln_g, lam_init):
    B, W, S = qt.shape
    tile = min(S, ATTN_TILE)
    nq = S // tile
    n_heads = W // HEAD_W
    qi_tab = jnp.asarray([qi for qi in range(nq) for _ in range(qi + 1)], jnp.int32)
    ki_tab = jnp.asarray([ki for qi in range(nq) for ki in range(qi + 1)], jnp.int32)
    lam_blk = pl.BlockSpec((1, HEAD_DIM), lambda b, t, qt_, kt_: (0, 0))
    grid_spec = pltpu.PrefetchScalarGridSpec(
        num_scalar_prefetch=2,
        grid=(B, int(qi_tab.shape[0])),
        in_specs=[pl.BlockSpec((None, W, tile), lambda b, t, qt_, kt_: (b, 0, qt_[t])),
                  pl.BlockSpec((None, tile, W), lambda b, t, qt_, kt_: (b, kt_[t], 0)),
                  pl.BlockSpec((None, W, tile), lambda b, t, qt_, kt_: (b, 0, kt_[t])),
                  lam_blk, lam_blk, lam_blk, lam_blk,
                  pl.BlockSpec((HEAD_W, 1), lambda b, t, qt_, kt_: (0, 0))],
        out_specs=pl.BlockSpec((None, tile, W), lambda b, t, qt_, kt_: (b, qt_[t], 0)),
        scratch_shapes=[pltpu.VMEM((n_heads, HEAD_W, 2 * tile), BF16),
                        pltpu.VMEM((n_heads, 1, 2 * tile), F32),
                        pltpu.VMEM((n_heads, 1, 2 * tile), F32),
                        pltpu.VMEM((n_heads, HEAD_W, 2 * tile), F32),
                        pltpu.VMEM((2, tile, 2 * tile), F32)])
    kern = functools.partial(_prompt_attn_kernel, tile=tile, lam_init=lam_init)
    return pl.pallas_call(kern, grid_spec=grid_spec, out_shape=jax.ShapeDtypeStruct((B, S, W), BF16),
                          compiler_params=_params(2), name="prompt_attention")(
        qi_tab, ki_tab, qt, k, vt, *[x.reshape(1, HEAD_DIM) for x in lams], subln_g.reshape(HEAD_W, 1))


def _key_page_copy(ck_hbm, page, buf_ref, slot, j, sem):
    n = ck_hbm.shape[2]
    return pltpu.make_async_copy(ck_hbm.at[page], buf_ref.at[slot, :, pl.ds(j * n, n)], sem.at[slot])


def _value_page_copy(cv_hbm, page, buf_ref, slot, j, sem):
    n = cv_hbm.shape[1]
    return pltpu.make_async_copy(cv_hbm.at[page], buf_ref.at[slot, pl.ds(j * n, n), :], sem.at[slot])


def _sample_attn_kernel(pt_ref, q_ref, kn_ref, vn_ref, lq1_ref, lk1_ref, lq2_ref, lk2_ref, g_ref, ck_hbm, cv_hbm,
                        o_ref, kbuf_ref, vbuf_ref, s_ref, p_ref, ksem, vsem, *, n_pages, lam_init):
    b = pl.program_id(0)
    n_seq = pl.num_programs(0)
    page_rows = ck_hbm.shape[2]
    chunk = PAGES_PER_CHUNK * page_rows
    n_chunks = n_pages // PAGES_PER_CHUNK
    units = 2 * n_chunks
    past = n_pages * page_rows
    L, W = q_ref.shape
    rows_per_head = 2 * L

    def unit_copies(seq, i, slot):
        c = i % n_chunks
        pages = [pt_ref[seq * n_pages + c * PAGES_PER_CHUNK + j] for j in range(PAGES_PER_CHUNK)]
        if i < n_chunks:
            return [_key_page_copy(ck_hbm, pg, kbuf_ref, slot, j, ksem) for j, pg in enumerate(pages)]
        return [_value_page_copy(cv_hbm, pg, vbuf_ref, slot, j, vsem) for j, pg in enumerate(pages)]

    def start_unit(seq, i):
        for cp in unit_copies(seq, i, i % RING_SLOTS):
            cp.start()

    @pl.when(b == 0)
    def _():
        for i in range(RING_SLOTS - 1):
            start_unit(b, i)

    q = q_ref[...]
    q2 = jnp.concatenate([q] * (2 * N_HEADS), axis=0)
    r_id = lax.broadcasted_iota(jnp.int32, q2.shape, 0) // L
    c_id = lax.broadcasted_iota(jnp.int32, q2.shape, 1) // HEAD_DIM
    q2 = jnp.where(r_id == c_id, q2, 0.0).astype(BF16)
    pad = jnp.zeros((LANES - L, W), F32)
    kn = jnp.concatenate([kn_ref[...], pad], axis=0).astype(BF16)
    vn = jnp.concatenate([vn_ref[...], pad], axis=0).astype(BF16)

    acc = [jnp.zeros((rows_per_head, HEAD_W), F32) for _ in range(N_HEADS)]
    for i in range(units):
        slot = i % RING_SLOTS
        nxt = i + RING_SLOTS - 1
        if nxt < units:
            start_unit(b, nxt)
        else:
            @pl.when(b + 1 < n_seq)
            def _():
                start_unit(b + 1, nxt - units)
        for cp in unit_copies(b, i, slot):
            cp.wait()
        if i < n_chunks:
            kt = kbuf_ref[slot].astype(BF16)
            s_ref[:, i * chunk:(i + 1) * chunk] = jnp.dot(q2, kt, preferred_element_type=F32)
            if i == n_chunks - 1:
                sn = lax.dot_general(q2, kn, NT_DIMS, preferred_element_type=F32)
                key = lax.broadcasted_iota(jnp.int32, sn.shape, 1)
                qry = lax.broadcasted_iota(jnp.int32, sn.shape, 0) % L
                s_ref[:, past:] = jnp.where(key <= qry, sn, NEG_INF)
                s = s_ref[...]
                m = jnp.max(s, axis=1, keepdims=True)
                p = jnp.exp2(s - m)
                l = jnp.sum(p, axis=1, keepdims=True)
                p_ref[...] = p.astype(BF16)
        else:
            c = i - n_chunks
            for h in range(N_HEADS):
                vh = vbuf_ref[slot, pl.ds(h, chunk, stride=N_HEADS), :].astype(BF16)
                ph = p_ref[h * rows_per_head:(h + 1) * rows_per_head, c * chunk:(c + 1) * chunk]
                acc[h] = acc[h] + jnp.dot(ph, vh, preferred_element_type=F32)
    lam = _lambda(lq1_ref, lk1_ref, lq2_ref, lk2_ref, lam_init)
    outs = []
    for h in range(N_HEADS):
        rows = slice(h * rows_per_head, (h + 1) * rows_per_head)
        a = acc[h] + jnp.dot(p_ref[rows, past:], vn[:, h * HEAD_W:(h + 1) * HEAD_W], preferred_element_type=F32)
        on = a / l[rows, :]
        outs.append(_head_norm(on[:L, :] - lam * on[L:, :], g_ref[...], lam_init))
    o_ref[...] = jnp.concatenate(outs, axis=1)


def _sample_attention(q, k_new, v_new, ck, cv, page_table, lams, subln_g, lam_init):
    B, L, W = q.shape
    n_pages = page_table.shape[1]
    page_rows = ck.shape[2]
    chunk = PAGES_PER_CHUNK * page_rows
    past = n_pages * page_rows
    n_rows = 2 * N_HEADS * L
    new_blk = pl.BlockSpec((None, L, W), lambda b, pt: (b, 0, 0))
    lam_blk = pl.BlockSpec((1, HEAD_DIM), lambda b, pt: (0, 0))
    grid_spec = pltpu.PrefetchScalarGridSpec(
        num_scalar_prefetch=1,
        grid=(B,),
        in_specs=[new_blk, new_blk, new_blk, lam_blk, lam_blk, lam_blk, lam_blk,
                  pl.BlockSpec((1, HEAD_W), lambda b, pt: (0, 0)),
                  pl.BlockSpec(memory_space=pl.ANY), pl.BlockSpec(memory_space=pl.ANY)],
        out_specs=new_blk,
        scratch_shapes=[pltpu.VMEM((RING_SLOTS, W, chunk), ck.dtype),
                        pltpu.VMEM((RING_SLOTS, chunk * N_HEADS, HEAD_W), cv.dtype),
                        pltpu.VMEM((n_rows, past + LANES), F32),
                        pltpu.VMEM((n_rows, past + LANES), BF16),
                        pltpu.SemaphoreType.DMA((RING_SLOTS,)),
                        pltpu.SemaphoreType.DMA((RING_SLOTS,))])
    kern = functools.partial(_sample_attn_kernel, n_pages=n_pages, lam_init=lam_init)
    return pl.pallas_call(kern, grid_spec=grid_spec, out_shape=jax.ShapeDtypeStruct((B, L, W), F32),
                          compiler_params=_params(1), name="sample_attention")(
        page_table.reshape(-1), q, k_new, v_new, *[x.reshape(1, HEAD_DIM) for x in lams],
        subln_g.reshape(1, HEAD_W), ck, cv)


def _mlp_kernel(x_ref, yp_ref, o_ref, g1_ref, sh2_ref, sc2_ref, g2_ref, n2_ref, nf_ref, wo_ref, w1_ref, w2_ref,
                y_ref, *, final_norm):
    G, R, D = x_ref.shape
    x = x_ref[...]
    cat = jnp.concatenate([yp_ref[...].reshape(G * R, -1).astype(BF16), o_ref[...].reshape(G * R, -1).astype(BF16)],
                          axis=1)
    mix = jnp.dot(cat, wo_ref[...], preferred_element_type=F32)
    x = x + g1_ref[...] * mix.reshape(G, R, D)
    h = (_rms(x, n2_ref[...]) * (1.0 + sc2_ref[...]) + sh2_ref[...]).reshape(G * R, D).astype(BF16)
    acc = jnp.zeros((G * R, D), F32)
    for c in range(w1_ref.shape[1] // FF_CHUNK):
        f = jnp.dot(h, w1_ref[:, c * FF_CHUNK:(c + 1) * FF_CHUNK], preferred_element_type=F32)
        f = jnp.square(jnp.maximum(f, 0.0)).astype(BF16)
        acc = acc + jnp.dot(f, w2_ref[c * FF_CHUNK:(c + 1) * FF_CHUNK, :], preferred_element_type=F32)
    x = x + g2_ref[...] * acc.reshape(G, R, D)
    if final_norm:
        x = _rms(x, nf_ref[...])
    y_ref[...] = x


def _mlp(x, y_pool, o, mod, norm2_g, final_g, w_out, w1, w2, *, group_rows, final_norm):
    B, L, D = x.shape
    if L >= group_rows:
        G, R = 1, group_rows
    else:
        G, R = group_rows // L, L
    grid = (B // G, L // R)
    blk = lambda w: pl.BlockSpec((G, R, w), lambda b, l: (b, l, 0))
    mod_blk = lambda j: pl.BlockSpec((G, 1, D), lambda b, l: (b, 0, j))
    kern = functools.partial(_mlp_kernel, final_norm=final_norm)
    return pl.pallas_call(
        kern, grid=grid,
        in_specs=[blk(D), blk(y_pool.shape[2]), blk(o.shape[2]), mod_blk(2), mod_blk(3), mod_blk(4), mod_blk(5),
                  _resident((1, D)), _resident((1, D)), _resident(w_out.shape), _resident(w1.shape),
                  _resident(w2.shape)],
        out_specs=blk(D),
        out_shape=jax.ShapeDtypeStruct((B, L, D), F32),
        compiler_params=_params(2), name="mlp")(
        x, y_pool, o, mod, mod, mod, mod, norm2_g.reshape(1, D), final_g.reshape(1, D), w_out, w1, w2)


def _rope_angles(pos):
    inv_freq = ROPE_THETA ** (-jnp.arange(HALF, dtype=F32) * 2.0 / HEAD_DIM)
    ang = pos.astype(F32)[:, None] * inv_freq[None, :]
    return jnp.cos(ang), jnp.sin(ang)


def _rope_row_tables(cos, sin):
    reps = LANES // HEAD_DIM
    return jnp.tile(cos, (1, 2 * reps)), jnp.tile(jnp.concatenate([-sin, sin], axis=1), (1, reps))


def _block_diag(w):
    g, c, d = w.shape
    out = jnp.zeros((g * c, g * d), w.dtype)
    for i in range(g):
        out = out.at[i * c:(i + 1) * c, i * d:(i + 1) * d].set(w[i])
    return out


def kernel(x_prompt, x_sample, c_prompt, c_sample, cache_k, cache_v, state_pool, page_table, w_ada, b_ada, norm1_g, norm2_g, w_in, pool_w, pool_scale, lambda_q1, lambda_k1, lambda_q2, lambda_k2, subln_g, w_out, w_mlp1, w_mlp2, final_g):
    depth = w_ada.shape[0]
    B, S, D = x_prompt.shape
    NB, L, _ = x_sample.shape
    n_pages = page_table.shape[1]
    n_pool_pages, page_rows = cache_k.shape[1], cache_k.shape[2]
    past_len = n_pages * page_rows
    attn_w = cache_k.shape[3] * cache_k.shape[4]
    pool_cols = pool_scale.shape[1]

    cos_p, sin_p = _rope_angles(jnp.arange(S, dtype=jnp.int32))
    cos_s, sin_s = _rope_row_tables(*_rope_angles(past_len + jnp.arange(L, dtype=jnp.int32)))
    cos_s, sin_s = jnp.tile(cos_s, (NB, 1)), jnp.tile(sin_s, (NB, 1))
    c_all = jnp.concatenate([c_prompt, c_sample], axis=0)

    xp, xs = x_prompt, x_sample
    kp_l, vp_l, pp_l, ks_l, vs_l, ps_l = [], [], [], [], [], []
    for l in range(depth):
        lam_init = 0.8 - 0.6 * math.exp(-0.3 * l)
        lams = (lambda_q1[l], lambda_k1[l], lambda_q2[l], lambda_k2[l])
        w_pool = w_in[l][:, :pool_cols].astype(BF16)
        w_qkv_t = w_in[l][:, pool_cols:].T.astype(BF16)
        w_out_b, w1_b, w2_b = w_out[l].astype(BF16), w_mlp1[l].astype(BF16), w_mlp2[l].astype(BF16)
        pool_bd = _block_diag(pool_w[l]).astype(BF16)
        last = l == depth - 1

        mod = _modulation(c_all, w_ada[l], b_ada[l])
        mod_p = mod[:B].reshape(B, 1, N_MOD * D)
        mod_s = mod[B:].reshape(NB, 1, N_MOD * D)

        kt, v, qtb, kb, vtb, yp, tail = _inproj_prompt(xp, mod_p, norm1_g[l], w_pool, w_qkv_t, cos_p.T, sin_p.T,
                                                       pool_bd, pool_scale[l])
        o = _prompt_attention(qtb, kb, vtb, lams, subln_g[l], lam_init)
        kp_l.append(kt.reshape(B, 2 * N_HEADS, HEAD_DIM, S).transpose(0, 3, 1, 2))
        vp_l.append(v.reshape(B, S, N_HEADS, HEAD_W))
        pp_l.append(tail[:, POOL_PRE - POOL_BUF:])
        xp = _mlp(xp, yp, o, mod_p, norm2_g[l], final_g, w_out_b, w1_b, w2_b, group_rows=TOKEN_TILE, final_norm=last)

        pre = jnp.pad(state_pool[l], ((0, 0), (POOL_PRE - POOL_BUF, 0), (0, 0)))
        k, v, qf, yp, tail = _inproj_sample(xs, mod_s, norm1_g[l], w_pool, w_qkv_t, cos_s, sin_s, pool_bd,
                                            pool_scale[l], pre, pos0=past_len)
        ck = cache_k[l].transpose(0, 2, 3, 1).reshape(n_pool_pages, attn_w, page_rows)
        cv = cache_v[l].reshape(n_pool_pages, page_rows * N_HEADS, HEAD_W)
        o = _sample_attention(qf, k, v, ck, cv, page_table, lams, subln_g[l], lam_init)
        ks_l.append(k.reshape(NB, L, 2 * N_HEADS, HEAD_DIM))
        vs_l.append(v.reshape(NB, L, N_HEADS, HEAD_W))
        ps_l.append(tail[:, POOL_PRE - POOL_BUF:])
        xs = _mlp(xs, yp, o, mod_s, norm2_g[l], final_g, w_out_b, w1_b, w2_b, group_rows=TOKEN_TILE, final_norm=last)

    return (xp, xs, jnp.stack(kp_l), jnp.stack(vp_l), jnp.stack(pp_l),
            jnp.stack(ks_l), jnp.stack(vs_l), jnp.stack(ps_l))
```

```python
import functools
import math

import jax
import jax.numpy as jnp
from jax import lax
from jax.experimental import pallas as pl
from jax.experimental.pallas import tpu as pltpu

N_HEADS = 4
HEAD_DIM = 64
HALF = HEAD_DIM // 2
HEAD_W = 2 * HEAD_DIM
POOL_WINDOWS = (2, 4, 8, 16)
POOL_BUF = max(POOL_WINDOWS) - 1
ROPE_THETA = 10000.0
EPS = 1e-6
NEG_INF = -1e30
SCALE = HEAD_DIM ** -0.5
Q_SCALE = SCALE * math.log2(math.e)
N_MOD = 6

LANES = 128
VMEM_LIMIT_BYTES = 56 * 1024 * 1024

POOL_PRE = 16
TOKEN_TILE = 512
ATTN_TILE = 512
ATTN_COLS = 256
PAGES_PER_CHUNK = 8
RING_SLOTS = 4
FF_CHUNK = 512

F32 = jnp.float32
BF16 = jnp.bfloat16
NT_DIMS = (((1,), (1,)), ((), ()))


def _rms(x, g):
    return x * lax.rsqrt(jnp.mean(x * x, axis=-1, keepdims=True) + EPS) * g


def _params(n_grid):
    return pltpu.CompilerParams(dimension_semantics=("arbitrary",) * n_grid,
                                vmem_limit_bytes=VMEM_LIMIT_BYTES)


def _resident(shape):
    nd = len(shape)
    return pl.BlockSpec(shape, lambda *_: (0,) * nd, pipeline_mode=pl.Buffered(1))


def _mod_kernel(c_ref, w_ref, b_ref, o_ref):
    c = c_ref[...]
    s = c * jax.nn.sigmoid(c)
    o_ref[...] = jnp.dot(s.astype(BF16), w_ref[...].astype(BF16), preferred_element_type=F32) + b_ref[...]


def _modulation(c, w_ada, b_ada):
    n, d = c.shape
    return pl.pallas_call(
        _mod_kernel,
        grid=(N_MOD,),
        in_specs=[pl.BlockSpec((n, d), lambda j: (0, 0)),
                  pl.BlockSpec((d, d), lambda j: (0, j)),
                  pl.BlockSpec((1, d), lambda j: (0, j))],
        out_specs=pl.BlockSpec((n, d), lambda j: (0, j)),
        out_shape=jax.ShapeDtypeStruct((n, N_MOD * d), F32),
        compiler_params=_params(1),
        name="modulation",
    )(c, w_ada, b_ada.reshape(1, -1))


def _rope_rows(t, cos, sin_signed, first_half):
    outs = []
    for c in range(t.shape[1] // LANES):
        tc = t[:, c * LANES:(c + 1) * LANES]
        rot = jnp.where(first_half, pltpu.roll(tc, LANES - HALF, 1), pltpu.roll(tc, HALF, 1))
        outs.append(tc * cos + rot * sin_signed)
    return jnp.concatenate(outs, axis=1)


def _rope_cols(t, cos, sin):
    outs = []
    for c in range(t.shape[0] // HEAD_DIM):
        x1 = t[c * HEAD_DIM:c * HEAD_DIM + HALF, :]
        x2 = t[c * HEAD_DIM + HALF:(c + 1) * HEAD_DIM, :]
        outs += [x1 * cos - x2 * sin, x2 * cos + x1 * sin]
    return jnp.concatenate(outs, axis=0)


def _pool_mix(u_pool, pre, pw_ref, ps_ref, pos_base, G, R):
    C = u_pool.shape[1]
    u3 = u_pool.reshape(G, R, C)
    ext3 = jnp.concatenate([pre, u3], axis=1)
    tail = ext3[:, R:R + POOL_PRE, :]
    ext = ext3.reshape(G * (POOL_PRE + R), C)
    pos = pos_base + lax.broadcasted_iota(jnp.int32, (G, R, LANES), 1)
    ds = []
    for gi, w in enumerate(POOL_WINDOWS):
        a = ext[:, gi * LANES:(gi + 1) * LANES]
        sft = 1
        while sft < w:
            a = a + pltpu.roll(a, sft, 0)
            sft *= 2
        win = a.reshape(G, POOL_PRE + R, LANES)[:, POOL_PRE:, :]
        cnt = jnp.minimum(w, pos + 1).astype(F32)
        d = win / cnt - u3[:, :, gi * LANES:(gi + 1) * LANES]
        ds.append(d.reshape(G * R, LANES))
    dcat = jnp.concatenate(ds, axis=1)
    y = jnp.dot(dcat.astype(BF16), pw_ref[...], preferred_element_type=F32) * ps_ref[...]
    return y, tail


def _inproj_prompt_kernel(x_ref, shift_ref, scale_ref, g_ref, wp_ref, wqkv_ref, cos_ref, sin_ref, pw_ref, ps_ref,
                          kt_ref, v_ref, qtb_ref, kb_ref, vtb_ref, yp_ref, tail_ref, pre_ref, *, attn_width):
    _, R, _ = x_ref.shape
    A = attn_width
    step = pl.program_id(1)

    @pl.when(step == 0)
    def _():
        pre_ref[...] = jnp.zeros_like(pre_ref)

    h = _rms(x_ref[...], g_ref[...]) * (1.0 + scale_ref[...]) + shift_ref[...]
    h = h.reshape(R, -1).astype(BF16)
    ut = lax.dot_general(wqkv_ref[...], h, NT_DIMS, preferred_element_type=F32)
    cos, sin = cos_ref[...], sin_ref[...]
    qt = _rope_cols(ut[:A], cos, sin) * Q_SCALE
    kt = _rope_cols(ut[A:2 * A], cos, sin)
    vt = ut[2 * A:]
    qtb_ref[...] = qt.reshape(qtb_ref.shape).astype(qtb_ref.dtype)
    kt_ref[...] = kt.reshape(kt_ref.shape)
    vtb_ref[...] = vt.reshape(vtb_ref.shape).astype(vtb_ref.dtype)
    kb_ref[...] = kt.T.reshape(kb_ref.shape).astype(kb_ref.dtype)
    n_heads = A // HEAD_W
    for hd in range(n_heads):
        v_ref[0, pl.ds(hd, R, stride=n_heads), :] = vt[hd * HEAD_W:(hd + 1) * HEAD_W, :].T

    u_pool = jnp.dot(h, wp_ref[...], preferred_element_type=F32)
    y, tail = _pool_mix(u_pool, pre_ref[...], pw_ref, ps_ref, step * R, 1, R)
    yp_ref[...] = y.reshape(yp_ref.shape).astype(yp_ref.dtype)
    tail_ref[...] = tail
    pre_ref[...] = tail


def _inproj_sample_kernel(x_ref, shift_ref, scale_ref, g_ref, wp_ref, wqkv_ref, cos_ref, sin_ref, pw_ref, ps_ref,
                          pre_ref, k_ref, v_ref, q_ref, yp_ref, tail_ref, *, attn_width, pos0):
    G, R, _ = x_ref.shape
    A = attn_width
    h = _rms(x_ref[...], g_ref[...]) * (1.0 + scale_ref[...]) + shift_ref[...]
    h = h.reshape(G * R, -1).astype(BF16)
    u = lax.dot_general(h, wqkv_ref[...], NT_DIMS, preferred_element_type=F32)
    lane = lax.broadcasted_iota(jnp.int32, (G * R, LANES), 1)
    first_half = (lane % HEAD_DIM) < HALF
    cos, sin_signed = cos_ref[...], sin_ref[...]
    q_ref[...] = (_rope_rows(u[:, :A], cos, sin_signed, first_half) * Q_SCALE).reshape(q_ref.shape)
    k_ref[...] = _rope_rows(u[:, A:2 * A], cos, sin_signed, first_half).reshape(k_ref.shape)
    v_ref[...] = u[:, 2 * A:].reshape(v_ref.shape)

    u_pool = jnp.dot(h, wp_ref[...], preferred_element_type=F32)
    y, tail = _pool_mix(u_pool, pre_ref[...], pw_ref, ps_ref, pos0, G, R)
    yp_ref[...] = y.reshape(yp_ref.shape)
    tail_ref[...] = tail


def _inproj_prompt(x, shift_scale, g, w_pool, w_qkv_t, cos_t, sin_t, pool_bd, pool_scale):
    B, S, D = x.shape
    C = pool_bd.shape[0]
    A = w_qkv_t.shape[0] // 3
    R = min(S, TOKEN_TILE)
    rows = lambda w: pl.BlockSpec((1, R, w), lambda b, l: (b, l, 0))
    cols = pl.BlockSpec((1, A, R), lambda b, l: (b, 0, l))
    mod_blk = lambda j: pl.BlockSpec((1, 1, D), lambda b, l: (b, 0, j))
    tab_blk = pl.BlockSpec((HALF, R), lambda b, l: (0, l))
    tail_blk = pl.BlockSpec((1, POOL_PRE, C), lambda b, l: (b, 0, 0))
    n_heads = A // HEAD_W
    v_rows = pl.BlockSpec((1, R * n_heads, HEAD_W), lambda b, l: (b, l, 0))
    out_shape = [jax.ShapeDtypeStruct((B, A, S), F32), jax.ShapeDtypeStruct((B, S * n_heads, HEAD_W), F32),
                 jax.ShapeDtypeStruct((B, A, S), BF16), jax.ShapeDtypeStruct((B, S, A), BF16),
                 jax.ShapeDtypeStruct((B, A, S), BF16), jax.ShapeDtypeStruct((B, S, C), BF16),
                 jax.ShapeDtypeStruct((B, POOL_PRE, C), F32)]
    return pl.pallas_call(
        functools.partial(_inproj_prompt_kernel, attn_width=A),
        grid=(B, S // R),
        in_specs=[rows(D), mod_blk(0), mod_blk(1), _resident((1, D)), _resident(w_pool.shape),
                  _resident(w_qkv_t.shape), tab_blk, tab_blk, _resident(pool_bd.shape), _resident((1, C))],
        out_specs=[cols, v_rows, cols, rows(A), cols, rows(C), tail_blk],
        out_shape=out_shape,
        scratch_shapes=[pltpu.VMEM((1, POOL_PRE, C), F32)],
        compiler_params=_params(2), name="inproj_prompt")(
        x, shift_scale, shift_scale, g.reshape(1, D), w_pool, w_qkv_t, cos_t, sin_t, pool_bd,
        pool_scale.reshape(1, C))


def _inproj_sample(x, shift_scale, g, w_pool, w_qkv_t, cos_rows, sin_rows, pool_bd, pool_scale, pre, *, pos0):
    B, L, D = x.shape
    C = pool_bd.shape[0]
    A = w_qkv_t.shape[0] // 3
    full = lambda *shape: pl.BlockSpec(shape, lambda i: (0,) * len(shape))
    mod_blk = lambda j: pl.BlockSpec((B, 1, D), lambda i: (0, 0, j))
    out_shape = [jax.ShapeDtypeStruct((B, L, A), F32)] * 3 + [jax.ShapeDtypeStruct((B, L, C), F32),
                                                               jax.ShapeDtypeStruct((B, POOL_PRE, C), F32)]
    return pl.pallas_call(
        functools.partial(_inproj_sample_kernel, attn_width=A, pos0=pos0),
        grid=(1,),
        in_specs=[full(B, L, D), mod_blk(0), mod_blk(1), full(1, D), full(*w_pool.shape), full(*w_qkv_t.shape),
                  full(B * L, LANES), full(B * L, LANES), full(*pool_bd.shape), full(1, C), full(B, POOL_PRE, C)],
        out_specs=[full(B, L, A)] * 3 + [full(B, L, C), full(B, POOL_PRE, C)],
        out_shape=out_shape,
        compiler_params=_params(1), name="inproj_sample")(
        x, shift_scale, shift_scale, g.reshape(1, D), w_pool, w_qkv_t, cos_rows, sin_rows, pool_bd,
        pool_scale.reshape(1, C), pre)


def _lambda(lq1_ref, lk1_ref, lq2_ref, lk2_ref, lam_init):
    return (jnp.exp(jnp.sum(lq1_ref[...] * lk1_ref[...], axis=1, keepdims=True))
            - jnp.exp(jnp.sum(lq2_ref[...] * lk2_ref[...], axis=1, keepdims=True)) + lam_init)


def _head_norm(o, g, lam_init):
    return _rms(o, g) * (1.0 - lam_init)


def _prompt_attn_kernel(qi_ref, ki_ref, qt_ref, k_ref, vt_ref, lq1_ref, lk1_ref, lq2_ref, lk2_ref, g_ref,
                        o_ref, qs_ref, m_ref, l_ref, acc_ref, s_ref, *, tile, lam_init):
    t = pl.program_id(1)
    qi = qi_ref[t]
    ki = ki_ref[t]
    n_heads = qs_ref.shape[0]

    @pl.when(ki == 0)
    def _():
        zeros = jnp.zeros((HEAD_DIM, tile), qs_ref.dtype)
        for h in range(n_heads):
            qs_ref[h, :HEAD_DIM, :tile] = qt_ref[h * HEAD_W:h * HEAD_W + HEAD_DIM, :]
            qs_ref[h, HEAD_DIM:, :tile] = zeros
            qs_ref[h, :HEAD_DIM, tile:] = zeros
            qs_ref[h, HEAD_DIM:, tile:] = qt_ref[h * HEAD_W + HEAD_DIM:(h + 1) * HEAD_W, :]
        m_ref[...] = jnp.full_like(m_ref, NEG_INF)
        l_ref[...] = jnp.zeros_like(l_ref)
        acc_ref[...] = jnp.zeros_like(acc_ref)

    def scores(h, masked):
        s = jnp.dot(k_ref[:, h * HEAD_W:(h + 1) * HEAD_W], qs_ref[h], preferred_element_type=F32)
        if masked:
            key = lax.broadcasted_iota(jnp.int32, s.shape, 0)
            qry = lax.broadcasted_iota(jnp.int32, s.shape, 1) % tile
            s = jnp.where(key <= qry, s, NEG_INF)
        s_ref[h % 2] = s
        return jnp.max(s, axis=0, keepdims=True)

    def accumulate(h, m_cur):
        m_prev = m_ref[h]
        m_new = jnp.maximum(m_prev, m_cur)
        alpha = jnp.exp2(m_prev - m_new)
        p = jnp.exp2(s_ref[h % 2] - m_new)
        l_ref[h] = alpha * l_ref[h] + jnp.sum(p, axis=0, keepdims=True)
        acc_ref[h] = alpha * acc_ref[h] + jnp.dot(vt_ref[h * HEAD_W:(h + 1) * HEAD_W, :], p.astype(BF16),
                                                  preferred_element_type=F32)
        m_ref[h] = m_new

    def step(masked):
        m_cur = [None] * n_heads
        for h in range(n_heads):
            m_cur[h] = scores(h, masked)
            if h > 0:
                accumulate(h - 1, m_cur[h - 1])
        accumulate(n_heads - 1, m_cur[n_heads - 1])

    @pl.when(ki < qi)
    def _():
        step(False)

    @pl.when(ki == qi)
    def _():
        step(True)
        lam = _lambda(lq1_ref, lk1_ref, lq2_ref, lk2_ref, lam_init)
        for h in range(n_heads):
            on = acc_ref[h] / l_ref[h]
            o = on[:, :tile] - lam * on[:, tile:]
            y = o * lax.rsqrt(jnp.mean(o * o, axis=0, keepdims=True) + EPS) * g_ref[...] * (1.0 - lam_init)
            o_ref[:, h * HEAD_W:(h + 1) * HEAD_W] = y.T.astype(o_ref.dtype)


def _prompt_attention(qt, k, vt, lams, subln_g, lam_init):
    B, W, S = qt.shape
    tile = min(S, ATTN_TILE)
    nq = S // tile
    n_heads = W // HEAD_W
    qi_tab = jnp.asarray([qi for qi in range(nq) for _ in range(qi + 1)], jnp.int32)
    ki_tab = jnp.asarray([ki for qi in range(nq) for ki in range(qi + 1)], jnp.int32)
    lam_blk = pl.BlockSpec((1, HEAD_DIM), lambda b, t, qt_, kt_: (0, 0))
    grid_spec = pltpu.PrefetchScalarGridSpec(
        num_scalar_prefetch=2,
        grid=(B, int(qi_tab.shape[0])),
        in_specs=[pl.BlockSpec((None, W, tile), lambda b, t, qt_, kt_: (b, 0, qt_[t])),
                  pl.BlockSpec((None, tile, W), lambda b, t, qt_, kt_: (b, kt_[t], 0)),
                  pl.BlockSpec((None, W, tile), lambda b, t, qt_, kt_: (b, 0, kt_[t])),
                  lam_blk, lam_blk, lam_blk, lam_blk,
                  pl.BlockSpec((HEAD_W, 1), lambda b, t, qt_, kt_: (0, 0))],
        out_specs=pl.BlockSpec((None, tile, W), lambda b, t, qt_, kt_: (b, qt_[t], 0)),
        scratch_shapes=[pltpu.VMEM((n_heads, HEAD_W, 2 * tile), BF16),
                        pltpu.VMEM((n_heads, 1, 2 * tile), F32),
                        pltpu.VMEM((n_heads, 1, 2 * tile), F32),
                        pltpu.VMEM((n_heads, HEAD_W, 2 * tile), F32),
                        pltpu.VMEM((2, tile, 2 * tile), F32)])
    kern = functools.partial(_prompt_attn_kernel, tile=tile, lam_init=lam_init)
    return pl.pallas_call(kern, grid_spec=grid_spec, out_shape=jax.ShapeDtypeStruct((B, S, W), BF16),
                          compiler_params=_params(2), name="prompt_attention")(
        qi_tab, ki_tab, qt, k, vt, *[x.reshape(1, HEAD_DIM) for x in lams], subln_g.reshape(HEAD_W, 1))


def _key_page_copy(ck_hbm, page, buf_ref, slot, j, sem):
    n = ck_hbm.shape[2]
    return pltpu.make_async_copy(ck_hbm.at[page], buf_ref.at[slot, :, pl.ds(j * n, n)], sem.at[slot])


def _value_page_copy(cv_hbm, page, buf_ref, slot, j, sem):
    n = cv_hbm.shape[1]
    return pltpu.make_async_copy(cv_hbm.at[page], buf_ref.at[slot, pl.ds(j * n, n), :], sem.at[slot])


def _sample_attn_mlp_kernel(pt_ref, q_ref, kn_ref, vn_ref, lq1_ref, lk1_ref, lq2_ref, lk2_ref, g_ref, ck_hbm, cv_hbm,
                            x_ref, yp_ref, op_ref, g1_ref, sh2_ref, sc2_ref, g2_ref, n2_ref, nf_ref, wo_ref, w1_ref,
                            w2_ref, o_ref, y_ref, kbuf_ref, vbuf_ref, s_ref, p_ref, ksem, vsem, x1_ref, h_ref,
                            ff_ref, *, n_pages, lam_init, final_norm):
    b = pl.program_id(0)
    n_seq = pl.num_programs(0)
    page_rows = ck_hbm.shape[2]
    chunk = PAGES_PER_CHUNK * page_rows
    n_chunks = n_pages // PAGES_PER_CHUNK
    units = 2 * n_chunks
    past = n_pages * page_rows
    L, W = q_ref.shape
    rows_per_head = 2 * L

    def unit_copies(seq, i, slot):
        c = i % n_chunks
        pages = [pt_ref[seq * n_pages + c * PAGES_PER_CHUNK + j] for j in range(PAGES_PER_CHUNK)]
        if i < n_chunks:
            return [_key_page_copy(ck_hbm, pg, kbuf_ref, slot, j, ksem) for j, pg in enumerate(pages)]
        return [_value_page_copy(cv_hbm, pg, vbuf_ref, slot, j, vsem) for j, pg in enumerate(pages)]

    def start_unit(seq, i):
        for cp in unit_copies(seq, i, i % RING_SLOTS):
            cp.start()

    @pl.when(b == 0)
    def _():
        for i in range(RING_SLOTS - 1):
            start_unit(b, i)

    q = q_ref[...]
    q2 = jnp.concatenate([q] * (2 * N_HEADS), axis=0)
    r_id = lax.broadcasted_iota(jnp.int32, q2.shape, 0) // L
    c_id = lax.broadcasted_iota(jnp.int32, q2.shape, 1) // HEAD_DIM
    q2 = jnp.where(r_id == c_id, q2, 0.0).astype(BF16)
    pad = jnp.zeros((LANES - L, W), F32)
    kn = jnp.concatenate([kn_ref[...], pad], axis=0).astype(BF16)
    vn = jnp.concatenate([vn_ref[...], pad], axis=0).astype(BF16)

    x1, hh = _mlp_front(x_ref[...][None], yp_ref[...], op_ref[...], g1_ref[...], sh2_ref[...], sc2_ref[...],
                        n2_ref[...], wo_ref)
    x1_ref[...] = x1[0]
    h_ref[...] = hh
    n_ff = w1_ref.shape[1] // FF_CHUNK
    ff_at = {}
    for c in range(n_ff):
        ff_at.setdefault(c * units // n_ff, []).append(c)

    acc = [jnp.zeros((rows_per_head, HEAD_W), F32) for _ in range(N_HEADS)]
    for i in range(units):
        slot = i % RING_SLOTS
        nxt = i + RING_SLOTS - 1
        if nxt < units:
            start_unit(b, nxt)
        else:
            @pl.when(b + 1 < n_seq)
            def _():
                start_unit(b + 1, nxt - units)
        for c in ff_at.get(i, []):
            part = _mlp_ff(h_ref[...], w1_ref, w2_ref, c)
            ff_ref[...] = part if c == 0 else ff_ref[...] + part
        for cp in unit_copies(b, i, slot):
            cp.wait()
        if i < n_chunks:
            kt = kbuf_ref[slot].astype(BF16)
            s_ref[:, i * chunk:(i + 1) * chunk] = jnp.dot(q2, kt, preferred_element_type=F32)
            if i == n_chunks - 1:
                sn = lax.dot_general(q2, kn, NT_DIMS, preferred_element_type=F32)
                key = lax.broadcasted_iota(jnp.int32, sn.shape, 1)
                qry = lax.broadcasted_iota(jnp.int32, sn.shape, 0) % L
                s_ref[:, past:] = jnp.where(key <= qry, sn, NEG_INF)
                s = s_ref[...]
                m = jnp.max(s, axis=1, keepdims=True)
                p = jnp.exp2(s - m)
                l = jnp.sum(p, axis=1, keepdims=True)
                p_ref[...] = p.astype(BF16)
        else:
            c = i - n_chunks
            for h in range(N_HEADS):
                vh = vbuf_ref[slot, pl.ds(h, chunk, stride=N_HEADS), :].astype(BF16)
                ph = p_ref[h * rows_per_head:(h + 1) * rows_per_head, c * chunk:(c + 1) * chunk]
                acc[h] = acc[h] + jnp.dot(ph, vh, preferred_element_type=F32)
    lam = _lambda(lq1_ref, lk1_ref, lq2_ref, lk2_ref, lam_init)
    outs = []
    for h in range(N_HEADS):
        rows = slice(h * rows_per_head, (h + 1) * rows_per_head)
        a = acc[h] + jnp.dot(p_ref[rows, past:], vn[:, h * HEAD_W:(h + 1) * HEAD_W], preferred_element_type=F32)
        on = a / l[rows, :]
        outs.append(_head_norm(on[:L, :] - lam * on[L:, :], g_ref[...], lam_init))
    o_ref[...] = jnp.concatenate(outs, axis=1)
    y_ref[...] = _mlp_back(x1_ref[...], ff_ref[...], g2_ref[...], nf_ref[...], final_norm)


def _sample_attention_prompt_mlp(q, k_new, v_new, ck, cv, page_table, lams, subln_g, lam_init,
                                 x, y_pool, o, mod, norm2_g, final_g, w_out, w1, w2, final_norm):
    B, L, W = q.shape
    PB, S, D = x.shape
    n_pages = page_table.shape[1]
    page_rows = ck.shape[2]
    chunk = PAGES_PER_CHUNK * page_rows
    past = n_pages * page_rows
    n_rows = 2 * N_HEADS * L
    TR = PB * S // B
    assert TR * B == PB * S and S % TR == 0 and TR % 8 == 0
    new_blk = pl.BlockSpec((None, L, W), lambda b, pt: (b, 0, 0))
    lam_blk = pl.BlockSpec((1, HEAD_DIM), lambda b, pt: (0, 0))
    row_blk = lambda w: pl.BlockSpec((TR, w), lambda b, pt: (b, 0))
    mod_blk = lambda j: pl.BlockSpec((None, 1, D), lambda b, pt: (b * TR // S, 0, j))
    grid_spec = pltpu.PrefetchScalarGridSpec(
        num_scalar_prefetch=1,
        grid=(B,),
        in_specs=[new_blk, new_blk, new_blk, lam_blk, lam_blk, lam_blk, lam_blk,
                  pl.BlockSpec((1, HEAD_W), lambda b, pt: (0, 0)),
                  pl.BlockSpec(memory_space=pl.ANY), pl.BlockSpec(memory_space=pl.ANY),
                  row_blk(D), row_blk(y_pool.shape[2]), row_blk(o.shape[2]),
                  mod_blk(2), mod_blk(3), mod_blk(4), mod_blk(5),
                  _resident((1, D)), _resident((1, D)), _resident(w_out.shape), _resident(w1.shape),
                  _resident(w2.shape)],
        out_specs=[new_blk, row_blk(D)],
        scratch_shapes=[pltpu.VMEM((RING_SLOTS, W, chunk), ck.dtype),
                        pltpu.VMEM((RING_SLOTS, chunk * N_HEADS, HEAD_W), cv.dtype),
                        pltpu.VMEM((n_rows, past + LANES), F32),
                        pltpu.VMEM((n_rows, past + LANES), BF16),
                        pltpu.SemaphoreType.DMA((RING_SLOTS,)),
                        pltpu.SemaphoreType.DMA((RING_SLOTS,)),
                        pltpu.VMEM((TR, D), F32),
                        pltpu.VMEM((TR, D), BF16),
                        pltpu.VMEM((TR, D), F32)])
    kern = functools.partial(_sample_attn_mlp_kernel, n_pages=n_pages, lam_init=lam_init, final_norm=final_norm)
    o_s, y = pl.pallas_call(
        kern, grid_spec=grid_spec,
        out_shape=[jax.ShapeDtypeStruct((B, L, W), F32), jax.ShapeDtypeStruct((PB * S, D), F32)],
        compiler_params=_params(1), name="sample_attention_prompt_mlp")(
        page_table.reshape(-1), q, k_new, v_new, *[t.reshape(1, HEAD_DIM) for t in lams],
        subln_g.reshape(1, HEAD_W), ck, cv,
        x.reshape(PB * S, D), y_pool.reshape(PB * S, -1), o.reshape(PB * S, -1), mod, mod, mod, mod,
        norm2_g.reshape(1, D), final_g.reshape(1, D), w_out, w1, w2)
    return o_s, y.reshape(PB, S, D)


def _mlp_front(x, yp, o, g1, sh2, sc2, n2, wo_ref):
    G, R, D = x.shape
    cat = jnp.concatenate([yp.reshape(G * R, -1).astype(BF16), o.reshape(G * R, -1).astype(BF16)], axis=1)
    mix = jnp.dot(cat, wo_ref[...], preferred_element_type=F32)
    x = x + g1 * mix.reshape(G, R, D)
    h = (_rms(x, n2) * (1.0 + sc2) + sh2).reshape(G * R, D).astype(BF16)
    return x, h


def _mlp_ff(h, w1_ref, w2_ref, c):
    f = jnp.dot(h, w1_ref[:, c * FF_CHUNK:(c + 1) * FF_CHUNK], preferred_element_type=F32)
    f = jnp.square(jnp.maximum(f, 0.0)).astype(BF16)
    return jnp.dot(f, w2_ref[c * FF_CHUNK:(c + 1) * FF_CHUNK, :], preferred_element_type=F32)


def _mlp_back(x, acc, g2, nf, final_norm):
    x = x + g2 * acc.reshape(x.shape)
    return _rms(x, nf) if final_norm else x


def _mlp_kernel(x_ref, yp_ref, o_ref, g1_ref, sh2_ref, sc2_ref, g2_ref, n2_ref, nf_ref, wo_ref, w1_ref, w2_ref,
                y_ref, *, final_norm):
    x, h = _mlp_front(x_ref[...], yp_ref[...], o_ref[...], g1_ref[...], sh2_ref[...], sc2_ref[...], n2_ref[...],
                      wo_ref)
    acc = jnp.zeros(h.shape, F32)
    for c in range(w1_ref.shape[1] // FF_CHUNK):
        acc = acc + _mlp_ff(h, w1_ref, w2_ref, c)
    y_ref[...] = _mlp_back(x, acc, g2_ref[...], nf_ref[...], final_norm)


def _mlp(x, y_pool, o, mod, norm2_g, final_g, w_out, w1, w2, *, group_rows, final_norm):
    B, L, D = x.shape
    if L >= group_rows:
        G, R = 1, group_rows
    else:
        G, R = group_rows // L, L
    grid = (B // G, L // R)
    blk = lambda w: pl.BlockSpec((G, R, w), lambda b, l: (b, l, 0))
    mod_blk = lambda j: pl.BlockSpec((G, 1, D), lambda b, l: (b, 0, j))
    kern = functools.partial(_mlp_kernel, final_norm=final_norm)
    return pl.pallas_call(
        kern, grid=grid,
        in_specs=[blk(D), blk(y_pool.shape[2]), blk(o.shape[2]), mod_blk(2), mod_blk(3), mod_blk(4), mod_blk(5),
                  _resident((1, D)), _resident((1, D)), _resident(w_out.shape), _resident(w1.shape),
                  _resident(w2.shape)],
        out_specs=blk(D),
        out_shape=jax.ShapeDtypeStruct((B, L, D), F32),
        compiler_params=_params(2), name="mlp")(
        x, y_pool, o, mod, mod, mod, mod, norm2_g.reshape(1, D), final_g.reshape(1, D), w_out, w1, w2)


def _rope_angles(pos):
    inv_freq = ROPE_THETA ** (-jnp.arange(HALF, dtype=F32) * 2.0 / HEAD_DIM)
    ang = pos.astype(F32)[:, None] * inv_freq[None, :]
    return jnp.cos(ang), jnp.sin(ang)


def _rope_row_tables(cos, sin):
    reps = LANES // HEAD_DIM
    return jnp.tile(cos, (1, 2 * reps)), jnp.tile(jnp.concatenate([-sin, sin], axis=1), (1, reps))


def _block_diag(w):
    g, c, d = w.shape
    out = jnp.zeros((g * c, g * d), w.dtype)
    for i in range(g):
        out = out.at[i * c:(i + 1) * c, i * d:(i + 1) * d].set(w[i])
    return out


def kernel(x_prompt, x_sample, c_prompt, c_sample, cache_k, cache_v, state_pool, page_table, w_ada, b_ada, norm1_g, norm2_g, w_in, pool_w, pool_scale, lambda_q1, lambda_k1, lambda_q2, lambda_k2, subln_g, w_out, w_mlp1, w_mlp2, final_g):
    depth = w_ada.shape[0]
    B, S, D = x_prompt.shape
    NB, L, _ = x_sample.shape
    n_pages = page_table.shape[1]
    n_pool_pages, page_rows = cache_k.shape[1], cache_k.shape[2]
    past_len = n_pages * page_rows
    attn_w = cache_k.shape[3] * cache_k.shape[4]
    pool_cols = pool_scale.shape[1]

    cos_p, sin_p = _rope_angles(jnp.arange(S, dtype=jnp.int32))
    cos_s, sin_s = _rope_row_tables(*_rope_angles(past_len + jnp.arange(L, dtype=jnp.int32)))
    cos_s, sin_s = jnp.tile(cos_s, (NB, 1)), jnp.tile(sin_s, (NB, 1))
    c_all = jnp.concatenate([c_prompt, c_sample], axis=0)

    xp, xs = x_prompt, x_sample
    kp_l, vp_l, pp_l, ks_l, vs_l, ps_l = [], [], [], [], [], []
    for l in range(depth):
        lam_init = 0.8 - 0.6 * math.exp(-0.3 * l)
        lams = (lambda_q1[l], lambda_k1[l], lambda_q2[l], lambda_k2[l])
        w_pool = w_in[l][:, :pool_cols].astype(BF16)
        w_qkv_t = w_in[l][:, pool_cols:].T.astype(BF16)
        w_out_b, w1_b, w2_b = w_out[l].astype(BF16), w_mlp1[l].astype(BF16), w_mlp2[l].astype(BF16)
        pool_bd = _block_diag(pool_w[l]).astype(BF16)
        last = l == depth - 1

        mod = _modulation(c_all, w_ada[l], b_ada[l])
        mod_p = mod[:B].reshape(B, 1, N_MOD * D)
        mod_s = mod[B:].reshape(NB, 1, N_MOD * D)

        kt, v, qtb, kb, vtb, yp, tail = _inproj_prompt(xp, mod_p, norm1_g[l], w_pool, w_qkv_t, cos_p.T, sin_p.T,
                                                       pool_bd, pool_scale[l])
        o_p = _prompt_attention(qtb, kb, vtb, lams, subln_g[l], lam_init)
        kp_l.append(kt.reshape(B, 2 * N_HEADS, HEAD_DIM, S).transpose(0, 3, 1, 2))
        vp_l.append(v.reshape(B, S, N_HEADS, HEAD_W))
        pp_l.append(tail[:, POOL_PRE - POOL_BUF:])

        pre = jnp.pad(state_pool[l], ((0, 0), (POOL_PRE - POOL_BUF, 0), (0, 0)))
        k, v, qf, yp_s, tail = _inproj_sample(xs, mod_s, norm1_g[l], w_pool, w_qkv_t, cos_s, sin_s, pool_bd,
                                              pool_scale[l], pre, pos0=past_len)
        ck = cache_k[l].transpose(0, 2, 3, 1).reshape(n_pool_pages, attn_w, page_rows)
        cv = cache_v[l].reshape(n_pool_pages, page_rows * N_HEADS, HEAD_W)
        o_s, xp = _sample_attention_prompt_mlp(qf, k, v, ck, cv, page_table, lams, subln_g[l], lam_init,
                                               xp, yp, o_p, mod_p, norm2_g[l], final_g, w_out_b, w1_b, w2_b, last)
        ks_l.append(k.reshape(NB, L, 2 * N_HEADS, HEAD_DIM))
        vs_l.append(v.reshape(NB, L, N_HEADS, HEAD_W))
        ps_l.append(tail[:, POOL_PRE - POOL_BUF:])
        xs = _mlp(xs, yp_s, o_s, mod_s, norm2_g[l], final_g, w_out_b, w1_b, w2_b, group_rows=TOKEN_TILE,
                  final_norm=last)

    return (xp, xs, jnp.stack(kp_l), jnp.stack(vp_l), jnp.stack(pp_l),
            jnp.stack(ks_l), jnp.stack(vs_l), jnp.stack(ps_l))
```

```python
import functools
import math

import jax
import jax.numpy as jnp
from jax import lax
from jax.experimental import pallas as pl
from jax.experimental.pallas import tpu as pltpu

N_HEADS = 4
HEAD_DIM = 64
HALF = HEAD_DIM // 2
HEAD_W = 2 * HEAD_DIM
POOL_WINDOWS = (2, 4, 8, 16)
POOL_BUF = max(POOL_WINDOWS) - 1
ROPE_THETA = 10000.0
EPS = 1e-6
NEG_INF = -1e30
SCALE = HEAD_DIM ** -0.5
Q_SCALE = SCALE * math.log2(math.e)
N_MOD = 6

LANES = 128
VMEM_LIMIT_BYTES = 56 * 1024 * 1024

POOL_PRE = 16
TOKEN_TILE = 512
ATTN_TILE = 512
SUM_ROWS = 16
PAGES_PER_CHUNK = 16
RING_SLOTS = 2
FF_CHUNK = 512

F32 = jnp.float32
BF16 = jnp.bfloat16
NT_DIMS = (((1,), (1,)), ((), ()))


def _rms(x, g):
    return x * lax.rsqrt(jnp.mean(x * x, axis=-1, keepdims=True) + EPS) * g


def _params(n_grid):
    return pltpu.CompilerParams(dimension_semantics=("arbitrary",) * n_grid,
                                vmem_limit_bytes=VMEM_LIMIT_BYTES)


def _resident(shape):
    nd = len(shape)
    return pl.BlockSpec(shape, lambda *_: (0,) * nd, pipeline_mode=pl.Buffered(1))


def _mod_kernel(c_ref, w_ref, b_ref, o_ref):
    c = c_ref[...]
    s = c * jax.nn.sigmoid(c)
    o_ref[...] = jnp.dot(s.astype(BF16), w_ref[...].astype(BF16), preferred_element_type=F32) + b_ref[...]


def _modulation(c, w_ada, b_ada):
    n, d = c.shape
    return pl.pallas_call(
        _mod_kernel,
        grid=(N_MOD,),
        in_specs=[pl.BlockSpec((n, d), lambda j: (0, 0)),
                  pl.BlockSpec((d, d), lambda j: (0, j)),
                  pl.BlockSpec((1, d), lambda j: (0, j))],
        out_specs=pl.BlockSpec((n, d), lambda j: (0, j)),
        out_shape=jax.ShapeDtypeStruct((n, N_MOD * d), F32),
        compiler_params=_params(1),
        name="modulation",
    )(c, w_ada, b_ada.reshape(1, -1))


def _rope_rows(t, cos, sin_signed, first_half):
    outs = []
    for c in range(t.shape[1] // LANES):
        tc = t[:, c * LANES:(c + 1) * LANES]
        rot = jnp.where(first_half, pltpu.roll(tc, LANES - HALF, 1), pltpu.roll(tc, HALF, 1))
        outs.append(tc * cos + rot * sin_signed)
    return jnp.concatenate(outs, axis=1)


def _rope_cols(t, cos, sin):
    outs = []
    for c in range(t.shape[0] // HEAD_DIM):
        x1 = t[c * HEAD_DIM:c * HEAD_DIM + HALF, :]
        x2 = t[c * HEAD_DIM + HALF:(c + 1) * HEAD_DIM, :]
        outs += [x1 * cos - x2 * sin, x2 * cos + x1 * sin]
    return jnp.concatenate(outs, axis=0)


def _pool_mix(u_pool, pre, pw_ref, ps_ref, pos_base, G, R):
    C = u_pool.shape[1]
    u3 = u_pool.reshape(G, R, C)
    ext3 = jnp.concatenate([pre, u3], axis=1)
    tail = ext3[:, R:R + POOL_PRE, :]
    ext = ext3.reshape(G * (POOL_PRE + R), C)
    pos = pos_base + lax.broadcasted_iota(jnp.int32, (G, R, LANES), 1)
    ds = []
    for gi, w in enumerate(POOL_WINDOWS):
        a = ext[:, gi * LANES:(gi + 1) * LANES]
        sft = 1
        while sft < w:
            a = a + pltpu.roll(a, sft, 0)
            sft *= 2
        win = a.reshape(G, POOL_PRE + R, LANES)[:, POOL_PRE:, :]
        cnt = jnp.minimum(w, pos + 1).astype(F32)
        d = win / cnt - u3[:, :, gi * LANES:(gi + 1) * LANES]
        ds.append(d.reshape(G * R, LANES))
    dcat = jnp.concatenate(ds, axis=1)
    y = jnp.dot(dcat.astype(BF16), pw_ref[...], preferred_element_type=F32) * ps_ref[...]
    return y, tail


def _inproj_prompt_kernel(x_ref, shift_ref, scale_ref, g_ref, wp_ref, wqkv_ref, cos_ref, sin_ref, pw_ref, ps_ref,
                          kt_ref, v_ref, qtb_ref, kb_ref, vtb_ref, yp_ref, tail_ref, pre_ref, *, attn_width):
    _, R, _ = x_ref.shape
    A = attn_width
    step = pl.program_id(1)

    @pl.when(step == 0)
    def _():
        pre_ref[...] = jnp.zeros_like(pre_ref)

    h = _rms(x_ref[...], g_ref[...]) * (1.0 + scale_ref[...]) + shift_ref[...]
    h = h.reshape(R, -1).astype(BF16)
    ut = lax.dot_general(wqkv_ref[...], h, NT_DIMS, preferred_element_type=F32)
    cos, sin = cos_ref[...], sin_ref[...]
    qt = _rope_cols(ut[:A], cos, sin) * Q_SCALE
    kt = _rope_cols(ut[A:2 * A], cos, sin)
    vt = ut[2 * A:]
    qtb_ref[...] = qt.reshape(qtb_ref.shape).astype(qtb_ref.dtype)
    kt_ref[...] = kt.reshape(kt_ref.shape)
    vtb_ref[...] = vt.reshape(vtb_ref.shape).astype(vtb_ref.dtype)
    kb_ref[...] = kt.T.reshape(kb_ref.shape).astype(kb_ref.dtype)
    n_heads = A // HEAD_W
    for hd in range(n_heads):
        v_ref[0, pl.ds(hd, R, stride=n_heads), :] = vt[hd * HEAD_W:(hd + 1) * HEAD_W, :].T

    u_pool = jnp.dot(h, wp_ref[...], preferred_element_type=F32)
    y, tail = _pool_mix(u_pool, pre_ref[...], pw_ref, ps_ref, step * R, 1, R)
    yp_ref[...] = y.reshape(yp_ref.shape).astype(yp_ref.dtype)
    tail_ref[...] = tail
    pre_ref[...] = tail


def _inproj_sample_kernel(x_ref, shift_ref, scale_ref, g_ref, wp_ref, wqkv_ref, cos_ref, sin_ref, pw_ref, ps_ref,
                          pre_ref, k_ref, v_ref, q_ref, yp_ref, tail_ref, *, attn_width, pos0):
    G, R, _ = x_ref.shape
    A = attn_width
    h = _rms(x_ref[...], g_ref[...]) * (1.0 + scale_ref[...]) + shift_ref[...]
    h = h.reshape(G * R, -1).astype(BF16)
    u = lax.dot_general(h, wqkv_ref[...], NT_DIMS, preferred_element_type=F32)
    lane = lax.broadcasted_iota(jnp.int32, (G * R, LANES), 1)
    first_half = (lane % HEAD_DIM) < HALF
    cos, sin_signed = cos_ref[...], sin_ref[...]
    q_ref[...] = (_rope_rows(u[:, :A], cos, sin_signed, first_half) * Q_SCALE).reshape(q_ref.shape)
    k_ref[...] = _rope_rows(u[:, A:2 * A], cos, sin_signed, first_half).reshape(k_ref.shape)
    v_ref[...] = u[:, 2 * A:].reshape(v_ref.shape)

    u_pool = jnp.dot(h, wp_ref[...], preferred_element_type=F32)
    y, tail = _pool_mix(u_pool, pre_ref[...], pw_ref, ps_ref, pos0, G, R)
    yp_ref[...] = y.reshape(yp_ref.shape)
    tail_ref[...] = tail


def _inproj_prompt(x, shift_scale, g, w_pool, w_qkv_t, cos_t, sin_t, pool_bd, pool_scale):
    B, S, D = x.shape
    C = pool_bd.shape[0]
    A = w_qkv_t.shape[0] // 3
    R = min(S, TOKEN_TILE)
    rows = lambda w: pl.BlockSpec((1, R, w), lambda b, l: (b, l, 0))
    cols = pl.BlockSpec((1, A, R), lambda b, l: (b, 0, l))
    mod_blk = lambda j: pl.BlockSpec((1, 1, D), lambda b, l: (b, 0, j))
    tab_blk = pl.BlockSpec((HALF, R), lambda b, l: (0, l))
    tail_blk = pl.BlockSpec((1, POOL_PRE, C), lambda b, l: (b, 0, 0))
    n_heads = A // HEAD_W
    v_rows = pl.BlockSpec((1, R * n_heads, HEAD_W), lambda b, l: (b, l, 0))
    out_shape = [jax.ShapeDtypeStruct((B, A, S), F32), jax.ShapeDtypeStruct((B, S * n_heads, HEAD_W), F32),
                 jax.ShapeDtypeStruct((B, A, S), BF16), jax.ShapeDtypeStruct((B, S, A), BF16),
                 jax.ShapeDtypeStruct((B, A, S), BF16), jax.ShapeDtypeStruct((B, S, C), BF16),
                 jax.ShapeDtypeStruct((B, POOL_PRE, C), F32)]
    return pl.pallas_call(
        functools.partial(_inproj_prompt_kernel, attn_width=A),
        grid=(B, S // R),
        in_specs=[rows(D), mod_blk(0), mod_blk(1), _resident((1, D)), _resident(w_pool.shape),
                  _resident(w_qkv_t.shape), tab_blk, tab_blk, _resident(pool_bd.shape), _resident((1, C))],
        out_specs=[cols, v_rows, cols, rows(A), cols, rows(C), tail_blk],
        out_shape=out_shape,
        scratch_shapes=[pltpu.VMEM((1, POOL_PRE, C), F32)],
        compiler_params=_params(2), name="inproj_prompt")(
        x, shift_scale, shift_scale, g.reshape(1, D), w_pool, w_qkv_t, cos_t, sin_t, pool_bd,
        pool_scale.reshape(1, C))


def _inproj_sample(x, shift_scale, g, w_pool, w_qkv_t, cos_rows, sin_rows, pool_bd, pool_scale, pre, *, pos0):
    B, L, D = x.shape
    C = pool_bd.shape[0]
    A = w_qkv_t.shape[0] // 3
    full = lambda *shape: pl.BlockSpec(shape, lambda i: (0,) * len(shape))
    mod_blk = lambda j: pl.BlockSpec((B, 1, D), lambda i: (0, 0, j))
    out_shape = [jax.ShapeDtypeStruct((B, L, A), F32)] * 3 + [jax.ShapeDtypeStruct((B, L, C), F32),
                                                               jax.ShapeDtypeStruct((B, POOL_PRE, C), F32)]
    return pl.pallas_call(
        functools.partial(_inproj_sample_kernel, attn_width=A, pos0=pos0),
        grid=(1,),
        in_specs=[full(B, L, D), mod_blk(0), mod_blk(1), full(1, D), full(*w_pool.shape), full(*w_qkv_t.shape),
                  full(B * L, LANES), full(B * L, LANES), full(*pool_bd.shape), full(1, C), full(B, POOL_PRE, C)],
        out_specs=[full(B, L, A)] * 3 + [full(B, L, C), full(B, POOL_PRE, C)],
        out_shape=out_shape,
        compiler_params=_params(1), name="inproj_sample")(
        x, shift_scale, shift_scale, g.reshape(1, D), w_pool, w_qkv_t, cos_rows, sin_rows, pool_bd,
        pool_scale.reshape(1, C), pre)


def _lambda(lq1_ref, lk1_ref, lq2_ref, lk2_ref, lam_init):
    return (jnp.exp(jnp.sum(lq1_ref[...] * lk1_ref[...], axis=1, keepdims=True))
            - jnp.exp(jnp.sum(lq2_ref[...] * lk2_ref[...], axis=1, keepdims=True)) + lam_init)


def _head_norm(o, g, lam_init):
    return _rms(o, g) * (1.0 - lam_init)


def _prompt_attn_kernel(qi_ref, ki_ref, qt_ref, k_ref, vt_ref, lq1_ref, lk1_ref, lq2_ref, lk2_ref, g_ref,
                        o_ref, qs_ref, m_ref, acc_ref, s_ref, *, tile, lam_init):
    t = pl.program_id(1)
    qi = qi_ref[t]
    ki = ki_ref[t]
    n_heads = qs_ref.shape[0]

    @pl.when(ki == 0)
    def _():
        zeros = jnp.zeros((HEAD_DIM, tile), qs_ref.dtype)
        for h in range(n_heads):
            qs_ref[h, :HEAD_DIM, :tile] = qt_ref[h * HEAD_W:h * HEAD_W + HEAD_DIM, :]
            qs_ref[h, HEAD_DIM:, :tile] = zeros
            qs_ref[h, :HEAD_DIM, tile:] = zeros
            qs_ref[h, HEAD_DIM:, tile:] = qt_ref[h * HEAD_W + HEAD_DIM:(h + 1) * HEAD_W, :]
        m_ref[...] = jnp.full_like(m_ref, NEG_INF)
        acc_ref[...] = jnp.zeros_like(acc_ref)

    def scores(h, masked):
        s = jnp.dot(k_ref[:, h * HEAD_W:(h + 1) * HEAD_W], qs_ref[h], preferred_element_type=F32)
        if masked:
            key = lax.broadcasted_iota(jnp.int32, s.shape, 0)
            qry = lax.broadcasted_iota(jnp.int32, s.shape, 1) % tile
            s = jnp.where(key <= qry, s, NEG_INF)
        s_ref[h % 2] = s
        return jnp.max(s, axis=0, keepdims=True)

    def accumulate(h, m_cur):
        m_prev = m_ref[h]
        m_new = jnp.maximum(m_prev, m_cur)
        alpha = jnp.exp2(m_prev - m_new)
        p = jnp.exp2(s_ref[h % 2] - m_new)
        vt_ext = jnp.concatenate([vt_ref[h * HEAD_W:(h + 1) * HEAD_W, :], jnp.ones((SUM_ROWS, tile), BF16)], axis=0)
        acc_ref[h] = alpha * acc_ref[h] + jnp.dot(vt_ext, p.astype(BF16), preferred_element_type=F32)
        m_ref[h] = m_new

    def step(masked):
        m_cur = [None] * n_heads
        for h in range(n_heads):
            m_cur[h] = scores(h, masked)
            if h > 0:
                accumulate(h - 1, m_cur[h - 1])
        accumulate(n_heads - 1, m_cur[n_heads - 1])

    @pl.when(ki < qi)
    def _():
        step(False)

    @pl.when(ki == qi)
    def _():
        step(True)
        lam = _lambda(lq1_ref, lk1_ref, lq2_ref, lk2_ref, lam_init)
        for h in range(n_heads):
            on = acc_ref[h, :HEAD_W] / acc_ref[h, HEAD_W:HEAD_W + 1]
            o = on[:, :tile] - lam * on[:, tile:]
            y = o * lax.rsqrt(jnp.mean(o * o, axis=0, keepdims=True) + EPS) * g_ref[...] * (1.0 - lam_init)
            o_ref[:, h * HEAD_W:(h + 1) * HEAD_W] = y.T.astype(o_ref.dtype)


def _prompt_attention(qt, k, vt, lams, subln_g, lam_init):
    B, W, S = qt.shape
    tile = min(S, ATTN_TILE)
    nq = S // tile
    n_heads = W // HEAD_W
    qi_tab = jnp.asarray([qi for qi in range(nq) for _ in range(qi + 1)], jnp.int32)
    ki_tab = jnp.asarray([ki for qi in range(nq) for ki in range(qi + 1)], jnp.int32)
    lam_blk = pl.BlockSpec((1, HEAD_DIM), lambda b, t, qt_, kt_: (0, 0))
    grid_spec = pltpu.PrefetchScalarGridSpec(
        num_scalar_prefetch=2,
        grid=(B, int(qi_tab.shape[0])),
        in_specs=[pl.BlockSpec((None, W, tile), lambda b, t, qt_, kt_: (b, 0, qt_[t])),
                  pl.BlockSpec((None, tile, W), lambda b, t, qt_, kt_: (b, kt_[t], 0)),
                  pl.BlockSpec((None, W, tile), lambda b, t, qt_, kt_: (b, 0, kt_[t])),
                  lam_blk, lam_blk, lam_blk, lam_blk,
                  pl.BlockSpec((HEAD_W, 1), lambda b, t, qt_, kt_: (0, 0))],
        out_specs=pl.BlockSpec((None, tile, W), lambda b, t, qt_, kt_: (b, qt_[t], 0)),
        scratch_shapes=[pltpu.VMEM((n_heads, HEAD_W, 2 * tile), BF16),
                        pltpu.VMEM((n_heads, 1, 2 * tile), F32),
                        pltpu.VMEM((n_heads, HEAD_W + SUM_ROWS, 2 * tile), F32),
                        pltpu.VMEM((2, tile, 2 * tile), F32)])
    kern = functools.partial(_prompt_attn_kernel, tile=tile, lam_init=lam_init)
    return pl.pallas_call(kern, grid_spec=grid_spec, out_shape=jax.ShapeDtypeStruct((B, S, W), BF16),
                          compiler_params=_params(2), name="prompt_attention")(
        qi_tab, ki_tab, qt, k, vt, *[x.reshape(1, HEAD_DIM) for x in lams], subln_g.reshape(HEAD_W, 1))


def _key_page_copy(ck_hbm, page, buf_ref, slot, j, sem):
    n = ck_hbm.shape[2]
    return pltpu.make_async_copy(ck_hbm.at[page], buf_ref.at[slot, :, pl.ds(j * n, n)], sem.at[slot])


def _value_page_copy(cv_hbm, page, buf_ref, slot, j, sem):
    n = cv_hbm.shape[1]
    return pltpu.make_async_copy(cv_hbm.at[page], buf_ref.at[slot, pl.ds(j * n, n), :], sem.at[slot])


def _sample_attn_mlp_kernel(pt_ref, q_ref, kn_ref, vn_ref, lq1_ref, lk1_ref, lq2_ref, lk2_ref, g_ref, ck_hbm, cv_hbm,
                            x_ref, yp_ref, op_ref, g1_ref, sh2_ref, sc2_ref, g2_ref, n2_ref, nf_ref, wo_ref, w1_ref,
                            w2_ref, o_ref, y_ref, kbuf_ref, vbuf_ref, s_ref, p_ref, ksem, vsem, x1_ref, h_ref,
                            f_ref, ff_ref, *, n_pages, lam_init, final_norm):
    b = pl.program_id(0)
    n_seq = pl.num_programs(0)
    page_rows = ck_hbm.shape[2]
    chunk = PAGES_PER_CHUNK * page_rows
    n_chunks = n_pages // PAGES_PER_CHUNK
    units = 2 * n_chunks
    past = n_pages * page_rows
    L, W = q_ref.shape
    rows_per_head = 2 * L

    def unit_copies(seq, i, slot):
        c = i % n_chunks
        pages = [pt_ref[seq * n_pages + c * PAGES_PER_CHUNK + j] for j in range(PAGES_PER_CHUNK)]
        if i < n_chunks:
            return [_key_page_copy(ck_hbm, pg, kbuf_ref, slot, j, ksem) for j, pg in enumerate(pages)]
        return [_value_page_copy(cv_hbm, pg, vbuf_ref, slot, j, vsem) for j, pg in enumerate(pages)]

    def start_unit(seq, i):
        for cp in unit_copies(seq, i, i % RING_SLOTS):
            cp.start()

    @pl.when(b == 0)
    def _():
        for i in range(RING_SLOTS - 1):
            start_unit(b, i)

    q = q_ref[...]
    q2 = jnp.concatenate([q] * (2 * N_HEADS), axis=0)
    r_id = lax.broadcasted_iota(jnp.int32, q2.shape, 0) // L
    c_id = lax.broadcasted_iota(jnp.int32, q2.shape, 1) // HEAD_DIM
    q2 = jnp.where(r_id == c_id, q2, 0.0).astype(BF16)
    pad = jnp.zeros((LANES - L, W), F32)
    kn = jnp.concatenate([kn_ref[...], pad], axis=0).astype(BF16)
    vn = jnp.concatenate([vn_ref[...], pad], axis=0).astype(BF16)

    x1, hh = _mlp_front(x_ref[...][None], yp_ref[...], op_ref[...], g1_ref[...], sh2_ref[...], sc2_ref[...],
                        n2_ref[...], wo_ref)
    x1_ref[...] = x1[0]
    h_ref[...] = hh
    n_ff = w1_ref.shape[1] // FF_CHUNK
    mlp_at = {}
    for j in range(2 * n_ff):
        mlp_at.setdefault(j * units // (2 * n_ff), []).append(j)

    acc = [jnp.zeros((2 * rows_per_head, 2 * HEAD_W), F32) for _ in range(N_HEADS // 2)]
    for i in range(units):
        slot = i % RING_SLOTS
        nxt = i + RING_SLOTS - 1
        if nxt < units:
            start_unit(b, nxt)
        else:
            @pl.when(b + 1 < n_seq)
            def _():
                start_unit(b + 1, nxt - units)
        for j in mlp_at.get(i, []):
            c = j // 2
            if j % 2 == 0:
                f_ref[...] = _mlp_up(h_ref[...], w1_ref, c)
            else:
                part = _mlp_down(f_ref[...], w2_ref, c)
                ff_ref[...] = part if c == 0 else ff_ref[...] + part
        for cp in unit_copies(b, i, slot):
            cp.wait()
        if i < n_chunks:
            kt = kbuf_ref[slot].astype(BF16)
            s_ref[:, i * chunk:(i + 1) * chunk] = jnp.dot(q2, kt, preferred_element_type=F32)
            if i == n_chunks - 1:
                sn = lax.dot_general(q2, kn, NT_DIMS, preferred_element_type=F32)
                key = lax.broadcasted_iota(jnp.int32, sn.shape, 1)
                qry = lax.broadcasted_iota(jnp.int32, sn.shape, 0) % L
                s_ref[:, past:] = jnp.where(key <= qry, sn, NEG_INF)
                s = s_ref[...]
                m = jnp.max(s, axis=1, keepdims=True)
                p = jnp.exp2(s - m)
                l = jnp.sum(p, axis=1, keepdims=True)
                p_ref[...] = p.astype(BF16)
        else:
            c = i - n_chunks
            for hp in range(N_HEADS // 2):
                vpair = jnp.concatenate(
                    [vbuf_ref[slot, pl.ds(2 * hp + d, chunk, stride=N_HEADS), :].astype(BF16) for d in range(2)],
                    axis=1)
                ppair = p_ref[2 * hp * rows_per_head:(2 * hp + 2) * rows_per_head, c * chunk:(c + 1) * chunk]
                acc[hp] = acc[hp] + jnp.dot(ppair, vpair, preferred_element_type=F32)
    lam = _lambda(lq1_ref, lk1_ref, lq2_ref, lk2_ref, lam_init)
    outs = []
    for h in range(N_HEADS):
        rows = slice(h * rows_per_head, (h + 1) * rows_per_head)
        d = h % 2
        a = acc[h // 2][d * rows_per_head:(d + 1) * rows_per_head, d * HEAD_W:(d + 1) * HEAD_W]
        a = a + jnp.dot(p_ref[rows, past:], vn[:, h * HEAD_W:(h + 1) * HEAD_W], preferred_element_type=F32)
        on = a / l[rows, :]
        outs.append(_head_norm(on[:L, :] - lam * on[L:, :], g_ref[...], lam_init))
    o_ref[...] = jnp.concatenate(outs, axis=1)
    y_ref[...] = _mlp_back(x1_ref[...], ff_ref[...], g2_ref[...], nf_ref[...], final_norm)


def _sample_attention_prompt_mlp(q, k_new, v_new, ck, cv, page_table, lams, subln_g, lam_init,
                                 x, y_pool, o, mod, norm2_g, final_g, w_out, w1, w2, final_norm):
    B, L, W = q.shape
    PB, S, D = x.shape
    n_pages = page_table.shape[1]
    page_rows = ck.shape[2]
    chunk = PAGES_PER_CHUNK * page_rows
    past = n_pages * page_rows
    n_rows = 2 * N_HEADS * L
    TR = PB * S // B
    assert TR * B == PB * S and S % TR == 0 and TR % 8 == 0
    new_blk = pl.BlockSpec((None, L, W), lambda b, pt: (b, 0, 0))
    lam_blk = pl.BlockSpec((1, HEAD_DIM), lambda b, pt: (0, 0))
    row_blk = lambda w: pl.BlockSpec((TR, w), lambda b, pt: (b, 0))
    mod_blk = lambda j: pl.BlockSpec((None, 1, D), lambda b, pt: (b * TR // S, 0, j))
    grid_spec = pltpu.PrefetchScalarGridSpec(
        num_scalar_prefetch=1,
        grid=(B,),
        in_specs=[new_blk, new_blk, new_blk, lam_blk, lam_blk, lam_blk, lam_blk,
                  pl.BlockSpec((1, HEAD_W), lambda b, pt: (0, 0)),
                  pl.BlockSpec(memory_space=pl.ANY), pl.BlockSpec(memory_space=pl.ANY),
                  row_blk(D), row_blk(y_pool.shape[2]), row_blk(o.shape[2]),
                  mod_blk(2), mod_blk(3), mod_blk(4), mod_blk(5),
                  _resident((1, D)), _resident((1, D)), _resident(w_out.shape), _resident(w1.shape),
                  _resident(w2.shape)],
        out_specs=[new_blk, row_blk(D)],
        scratch_shapes=[pltpu.VMEM((RING_SLOTS, W, chunk), ck.dtype),
                        pltpu.VMEM((RING_SLOTS, chunk * N_HEADS, HEAD_W), cv.dtype),
                        pltpu.VMEM((n_rows, past + LANES), F32),
                        pltpu.VMEM((n_rows, past + LANES), BF16),
                        pltpu.SemaphoreType.DMA((RING_SLOTS,)),
                        pltpu.SemaphoreType.DMA((RING_SLOTS,)),
                        pltpu.VMEM((TR, D), F32),
                        pltpu.VMEM((TR, D), BF16),
                        pltpu.VMEM((TR, FF_CHUNK), BF16),
                        pltpu.VMEM((TR, D), F32)])
    kern = functools.partial(_sample_attn_mlp_kernel, n_pages=n_pages, lam_init=lam_init, final_norm=final_norm)
    o_s, y = pl.pallas_call(
        kern, grid_spec=grid_spec,
        out_shape=[jax.ShapeDtypeStruct((B, L, W), F32), jax.ShapeDtypeStruct((PB * S, D), F32)],
        compiler_params=_params(1), name="sample_attention_prompt_mlp")(
        page_table.reshape(-1), q, k_new, v_new, *[t.reshape(1, HEAD_DIM) for t in lams],
        subln_g.reshape(1, HEAD_W), ck, cv,
        x.reshape(PB * S, D), y_pool.reshape(PB * S, -1), o.reshape(PB * S, -1), mod, mod, mod, mod,
        norm2_g.reshape(1, D), final_g.reshape(1, D), w_out, w1, w2)
    return o_s, y.reshape(PB, S, D)


def _mlp_front(x, yp, o, g1, sh2, sc2, n2, wo_ref):
    G, R, D = x.shape
    cat = jnp.concatenate([yp.reshape(G * R, -1).astype(BF16), o.reshape(G * R, -1).astype(BF16)], axis=1)
    mix = jnp.dot(cat, wo_ref[...], preferred_element_type=F32)
    x = x + g1 * mix.reshape(G, R, D)
    h = (_rms(x, n2) * (1.0 + sc2) + sh2).reshape(G * R, D).astype(BF16)
    return x, h


def _mlp_up(h, w1_ref, c):
    f = jnp.dot(h, w1_ref[:, c * FF_CHUNK:(c + 1) * FF_CHUNK], preferred_element_type=F32)
    return jnp.square(jnp.maximum(f, 0.0)).astype(BF16)


def _mlp_down(f, w2_ref, c):
    return jnp.dot(f, w2_ref[c * FF_CHUNK:(c + 1) * FF_CHUNK, :], preferred_element_type=F32)


def _mlp_back(x, acc, g2, nf, final_norm):
    x = x + g2 * acc.reshape(x.shape)
    return _rms(x, nf) if final_norm else x


def _mlp_kernel(x_ref, yp_ref, o_ref, g1_ref, sh2_ref, sc2_ref, g2_ref, n2_ref, nf_ref, wo_ref, w1_ref, w2_ref,
                y_ref, *, final_norm):
    x, h = _mlp_front(x_ref[...], yp_ref[...], o_ref[...], g1_ref[...], sh2_ref[...], sc2_ref[...], n2_ref[...],
                      wo_ref)
    acc = jnp.zeros(h.shape, F32)
    for c in range(w1_ref.shape[1] // FF_CHUNK):
        acc = acc + _mlp_down(_mlp_up(h, w1_ref, c), w2_ref, c)
    y_ref[...] = _mlp_back(x, acc, g2_ref[...], nf_ref[...], final_norm)


def _mlp(x, y_pool, o, mod, norm2_g, final_g, w_out, w1, w2, *, group_rows, final_norm):
    B, L, D = x.shape
    if L >= group_rows:
        G, R = 1, group_rows
    else:
        G, R = group_rows // L, L
    grid = (B // G, L // R)
    blk = lambda w: pl.BlockSpec((G, R, w), lambda b, l: (b, l, 0))
    mod_blk = lambda j: pl.BlockSpec((G, 1, D), lambda b, l: (b, 0, j))
    kern = functools.partial(_mlp_kernel, final_norm=final_norm)
    return pl.pallas_call(
        kern, grid=grid,
        in_specs=[blk(D), blk(y_pool.shape[2]), blk(o.shape[2]), mod_blk(2), mod_blk(3), mod_blk(4), mod_blk(5),
                  _resident((1, D)), _resident((1, D)), _resident(w_out.shape), _resident(w1.shape),
                  _resident(w2.shape)],
        out_specs=blk(D),
        out_shape=jax.ShapeDtypeStruct((B, L, D), F32),
        compiler_params=_params(2), name="mlp")(
        x, y_pool, o, mod, mod, mod, mod, norm2_g.reshape(1, D), final_g.reshape(1, D), w_out, w1, w2)


def _rope_angles(pos):
    inv_freq = ROPE_THETA ** (-jnp.arange(HALF, dtype=F32) * 2.0 / HEAD_DIM)
    ang = pos.astype(F32)[:, None] * inv_freq[None, :]
    return jnp.cos(ang), jnp.sin(ang)


def _rope_row_tables(cos, sin):
    reps = LANES // HEAD_DIM
    return jnp.tile(cos, (1, 2 * reps)), jnp.tile(jnp.concatenate([-sin, sin], axis=1), (1, reps))


def _block_diag(w):
    g, c, d = w.shape
    out = jnp.zeros((g * c, g * d), w.dtype)
    for i in range(g):
        out = out.at[i * c:(i + 1) * c, i * d:(i + 1) * d].set(w[i])
    return out


def kernel(x_prompt, x_sample, c_prompt, c_sample, cache_k, cache_v, state_pool, page_table, w_ada, b_ada, norm1_g, norm2_g, w_in, pool_w, pool_scale, lambda_q1, lambda_k1, lambda_q2, lambda_k2, subln_g, w_out, w_mlp1, w_mlp2, final_g):
    depth = w_ada.shape[0]
    B, S, D = x_prompt.shape
    NB, L, _ = x_sample.shape
    n_pages = page_table.shape[1]
    n_pool_pages, page_rows = cache_k.shape[1], cache_k.shape[2]
    past_len = n_pages * page_rows
    attn_w = cache_k.shape[3] * cache_k.shape[4]
    pool_cols = pool_scale.shape[1]

    cos_p, sin_p = _rope_angles(jnp.arange(S, dtype=jnp.int32))
    cos_s, sin_s = _rope_row_tables(*_rope_angles(past_len + jnp.arange(L, dtype=jnp.int32)))
    cos_s, sin_s = jnp.tile(cos_s, (NB, 1)), jnp.tile(sin_s, (NB, 1))
    c_all = jnp.concatenate([c_prompt, c_sample], axis=0)

    xp, xs = x_prompt, x_sample
    kp_l, vp_l, pp_l, ks_l, vs_l, ps_l = [], [], [], [], [], []
    for l in range(depth):
        lam_init = 0.8 - 0.6 * math.exp(-0.3 * l)
        lams = (lambda_q1[l], lambda_k1[l], lambda_q2[l], lambda_k2[l])
        w_pool = w_in[l][:, :pool_cols].astype(BF16)
        w_qkv_t = w_in[l][:, pool_cols:].T.astype(BF16)
        w_out_b, w1_b, w2_b = w_out[l].astype(BF16), w_mlp1[l].astype(BF16), w_mlp2[l].astype(BF16)
        pool_bd = _block_diag(pool_w[l]).astype(BF16)
        last = l == depth - 1

        mod = _modulation(c_all, w_ada[l], b_ada[l])
        mod_p = mod[:B].reshape(B, 1, N_MOD * D)
        mod_s = mod[B:].reshape(NB, 1, N_MOD * D)

        kt, v, qtb, kb, vtb, yp, tail = _inproj_prompt(xp, mod_p, norm1_g[l], w_pool, w_qkv_t, cos_p.T, sin_p.T,
                                                       pool_bd, pool_scale[l])
        o_p = _prompt_attention(qtb, kb, vtb, lams, subln_g[l], lam_init)
        kp_l.append(kt.reshape(B, 2 * N_HEADS, HEAD_DIM, S).transpose(0, 3, 1, 2))
        vp_l.append(v.reshape(B, S, N_HEADS, HEAD_W))
        pp_l.append(tail[:, POOL_PRE - POOL_BUF:])

        pre = jnp.pad(state_pool[l], ((0, 0), (POOL_PRE - POOL_BUF, 0), (0, 0)))
        k, v, qf, yp_s, tail = _inproj_sample(xs, mod_s, norm1_g[l], w_pool, w_qkv_t, cos_s, sin_s, pool_bd,
                                              pool_scale[l], pre, pos0=past_len)
        ck = cache_k[l].transpose(0, 2, 3, 1).reshape(n_pool_pages, attn_w, page_rows)
        cv = cache_v[l].reshape(n_pool_pages, page_rows * N_HEADS, HEAD_W)
        o_s, xp = _sample_attention_prompt_mlp(qf, k, v, ck, cv, page_table, lams, subln_g[l], lam_init,
                                               xp, yp, o_p, mod_p, norm2_g[l], final_g, w_out_b, w1_b, w2_b, last)
        ks_l.append(k.reshape(NB, L, 2 * N_HEADS, HEAD_DIM))
        vs_l.append(v.reshape(NB, L, N_HEADS, HEAD_W))
        ps_l.append(tail[:, POOL_PRE - POOL_BUF:])
        xs = _mlp(xs, yp_s, o_s, mod_s, norm2_g[l], final_g, w_out_b, w1_b, w2_b, group_rows=TOKEN_TILE,
                  final_norm=last)

    return (xp, xs, jnp.stack(kp_l), jnp.stack(vp_l), jnp.stack(pp_l),
            jnp.stack(ks_l), jnp.stack(vs_l), jnp.stack(ps_l))
```

```python
import functools
import math

import jax
import jax.numpy as jnp
from jax import lax
from jax.experimental import pallas as pl
from jax.experimental.pallas import tpu as pltpu

N_HEADS = 4
HEAD_DIM = 64
HALF = HEAD_DIM // 2
HEAD_W = 2 * HEAD_DIM
POOL_WINDOWS = (2, 4, 8, 16)
POOL_BUF = max(POOL_WINDOWS) - 1
ROPE_THETA = 10000.0
EPS = 1e-6
NEG_INF = -1e30
SCALE = HEAD_DIM ** -0.5
Q_SCALE = SCALE * math.log2(math.e)
N_MOD = 6

LANES = 128
VMEM_LIMIT_BYTES = 56 * 1024 * 1024

POOL_PRE = 16
TOKEN_TILE = 512
ATTN_TILE = 512
SUM_ROWS = 16
PAGES_PER_CHUNK = 8
RING_SLOTS = 4
FF_CHUNK = 512

F32 = jnp.float32
BF16 = jnp.bfloat16
NT_DIMS = (((1,), (1,)), ((), ()))


def _rms(x, g):
    return x * lax.rsqrt(jnp.mean(x * x, axis=-1, keepdims=True) + EPS) * g


def _params(n_grid):
    return pltpu.CompilerParams(dimension_semantics=("arbitrary",) * n_grid,
                                vmem_limit_bytes=VMEM_LIMIT_BYTES)


def _resident(shape):
    nd = len(shape)
    return pl.BlockSpec(shape, lambda *_: (0,) * nd, pipeline_mode=pl.Buffered(1))


def _mod_kernel(c_ref, w_ref, b_ref, o_ref):
    c = c_ref[...]
    s = c * jax.nn.sigmoid(c)
    o_ref[...] = jnp.dot(s.astype(BF16), w_ref[...].astype(BF16), preferred_element_type=F32) + b_ref[...]


def _modulation(c, w_ada, b_ada):
    n, d = c.shape
    return pl.pallas_call(
        _mod_kernel,
        grid=(N_MOD,),
        in_specs=[pl.BlockSpec((n, d), lambda j: (0, 0)),
                  pl.BlockSpec((d, d), lambda j: (0, j)),
                  pl.BlockSpec((1, d), lambda j: (0, j))],
        out_specs=pl.BlockSpec((n, d), lambda j: (0, j)),
        out_shape=jax.ShapeDtypeStruct((n, N_MOD * d), F32),
        compiler_params=_params(1),
        name="modulation",
    )(c, w_ada, b_ada.reshape(1, -1))


def _rope_rows(t, cos, sin_signed, first_half):
    outs = []
    for c in range(t.shape[1] // LANES):
        tc = t[:, c * LANES:(c + 1) * LANES]
        rot = jnp.where(first_half, pltpu.roll(tc, LANES - HALF, 1), pltpu.roll(tc, HALF, 1))
        outs.append(tc * cos + rot * sin_signed)
    return jnp.concatenate(outs, axis=1)


def _rope_cols(t, cos, sin):
    outs = []
    for c in range(t.shape[0] // HEAD_DIM):
        x1 = t[c * HEAD_DIM:c * HEAD_DIM + HALF, :]
        x2 = t[c * HEAD_DIM + HALF:(c + 1) * HEAD_DIM, :]
        outs += [x1 * cos - x2 * sin, x2 * cos + x1 * sin]
    return jnp.concatenate(outs, axis=0)


def _pool_mix(u_pool, pre, pw_ref, ps_ref, pos_base, G, R):
    C = u_pool.shape[1]
    u3 = u_pool.reshape(G, R, C)
    ext3 = jnp.concatenate([pre, u3], axis=1)
    tail = ext3[:, R:R + POOL_PRE, :]
    ext = ext3.reshape(G * (POOL_PRE + R), C)
    pos = pos_base + lax.broadcasted_iota(jnp.int32, (G, R, LANES), 1)
    ds = []
    for gi, w in enumerate(POOL_WINDOWS):
        a = ext[:, gi * LANES:(gi + 1) * LANES]
        sft = 1
        while sft < w:
            a = a + pltpu.roll(a, sft, 0)
            sft *= 2
        win = a.reshape(G, POOL_PRE + R, LANES)[:, POOL_PRE:, :]
        cnt = jnp.minimum(w, pos + 1).astype(F32)
        d = win / cnt - u3[:, :, gi * LANES:(gi + 1) * LANES]
        ds.append(d.reshape(G * R, LANES))
    dcat = jnp.concatenate(ds, axis=1)
    y = jnp.dot(dcat.astype(BF16), pw_ref[...], preferred_element_type=F32) * ps_ref[...]
    return y, tail


def _inproj_prompt_kernel(x_ref, shift_ref, scale_ref, g_ref, wp_ref, wqkv_ref, cos_ref, sin_ref, pw_ref, ps_ref,
                          kt_ref, v_ref, qtb_ref, kb_ref, vtb_ref, yp_ref, tail_ref, pre_ref, *, attn_width):
    _, R, _ = x_ref.shape
    A = attn_width
    step = pl.program_id(1)

    @pl.when(step == 0)
    def _():
        pre_ref[...] = jnp.zeros_like(pre_ref)

    h = _rms(x_ref[...], g_ref[...]) * (1.0 + scale_ref[...]) + shift_ref[...]
    h = h.reshape(R, -1).astype(BF16)
    ut = lax.dot_general(wqkv_ref[...], h, NT_DIMS, preferred_element_type=F32)
    cos, sin = cos_ref[...], sin_ref[...]
    qt = _rope_cols(ut[:A], cos, sin) * Q_SCALE
    kt = _rope_cols(ut[A:2 * A], cos, sin)
    vt = ut[2 * A:]
    qtb_ref[...] = qt.reshape(qtb_ref.shape).astype(qtb_ref.dtype)
    kt_ref[...] = kt.reshape(kt_ref.shape)
    vtb_ref[...] = vt.reshape(vtb_ref.shape).astype(vtb_ref.dtype)
    kb_ref[...] = kt.T.reshape(kb_ref.shape).astype(kb_ref.dtype)
    n_heads = A // HEAD_W
    for hd in range(n_heads):
        v_ref[0, pl.ds(hd, R, stride=n_heads), :] = vt[hd * HEAD_W:(hd + 1) * HEAD_W, :].T

    u_pool = jnp.dot(h, wp_ref[...], preferred_element_type=F32)
    y, tail = _pool_mix(u_pool, pre_ref[...], pw_ref, ps_ref, step * R, 1, R)
    yp_ref[...] = y.reshape(yp_ref.shape).astype(yp_ref.dtype)
    tail_ref[...] = tail
    pre_ref[...] = tail


def _inproj_sample_kernel(x_ref, shift_ref, scale_ref, g_ref, wp_ref, wqkv_ref, cos_ref, sin_ref, pw_ref, ps_ref,
                          pre_ref, k_ref, v_ref, q_ref, yp_ref, tail_ref, *, attn_width, pos0):
    G, R, _ = x_ref.shape
    A = attn_width
    h = _rms(x_ref[...], g_ref[...]) * (1.0 + scale_ref[...]) + shift_ref[...]
    h = h.reshape(G * R, -1).astype(BF16)
    u = lax.dot_general(h, wqkv_ref[...], NT_DIMS, preferred_element_type=F32)
    lane = lax.broadcasted_iota(jnp.int32, (G * R, LANES), 1)
    first_half = (lane % HEAD_DIM) < HALF
    cos, sin_signed = cos_ref[...], sin_ref[...]
    q_ref[...] = (_rope_rows(u[:, :A], cos, sin_signed, first_half) * Q_SCALE).reshape(q_ref.shape)
    k_ref[...] = _rope_rows(u[:, A:2 * A], cos, sin_signed, first_half).reshape(k_ref.shape)
    v_ref[...] = u[:, 2 * A:].reshape(v_ref.shape)

    u_pool = jnp.dot(h, wp_ref[...], preferred_element_type=F32)
    y, tail = _pool_mix(u_pool, pre_ref[...], pw_ref, ps_ref, pos0, G, R)
    yp_ref[...] = y.reshape(yp_ref.shape)
    tail_ref[...] = tail


def _inproj_prompt(x, shift_scale, g, w_pool, w_qkv_t, cos_t, sin_t, pool_bd, pool_scale):
    B, S, D = x.shape
    C = pool_bd.shape[0]
    A = w_qkv_t.shape[0] // 3
    R = min(S, TOKEN_TILE)
    rows = lambda w: pl.BlockSpec((1, R, w), lambda b, l: (b, l, 0))
    cols = pl.BlockSpec((1, A, R), lambda b, l: (b, 0, l))
    mod_blk = lambda j: pl.BlockSpec((1, 1, D), lambda b, l: (b, 0, j))
    tab_blk = pl.BlockSpec((HALF, R), lambda b, l: (0, l))
    tail_blk = pl.BlockSpec((1, POOL_PRE, C), lambda b, l: (b, 0, 0))
    n_heads = A // HEAD_W
    v_rows = pl.BlockSpec((1, R * n_heads, HEAD_W), lambda b, l: (b, l, 0))
    out_shape = [jax.ShapeDtypeStruct((B, A, S), F32), jax.ShapeDtypeStruct((B, S * n_heads, HEAD_W), F32),
                 jax.ShapeDtypeStruct((B, A, S), BF16), jax.ShapeDtypeStruct((B, S, A), BF16),
                 jax.ShapeDtypeStruct((B, A, S), BF16), jax.ShapeDtypeStruct((B, S, C), BF16),
                 jax.ShapeDtypeStruct((B, POOL_PRE, C), F32)]
    return pl.pallas_call(
        functools.partial(_inproj_prompt_kernel, attn_width=A),
        grid=(B, S // R),
        in_specs=[rows(D), mod_blk(0), mod_blk(1), _resident((1, D)), _resident(w_pool.shape),
                  _resident(w_qkv_t.shape), tab_blk, tab_blk, _resident(pool_bd.shape), _resident((1, C))],
        out_specs=[cols, v_rows, cols, rows(A), cols, rows(C), tail_blk],
        out_shape=out_shape,
        scratch_shapes=[pltpu.VMEM((1, POOL_PRE, C), F32)],
        compiler_params=_params(2), name="inproj_prompt")(
        x, shift_scale, shift_scale, g.reshape(1, D), w_pool, w_qkv_t, cos_t, sin_t, pool_bd,
        pool_scale.reshape(1, C))


def _inproj_sample(x, shift_scale, g, w_pool, w_qkv_t, cos_rows, sin_rows, pool_bd, pool_scale, pre, *, pos0):
    B, L, D = x.shape
    C = pool_bd.shape[0]
    A = w_qkv_t.shape[0] // 3
    full = lambda *shape: pl.BlockSpec(shape, lambda i: (0,) * len(shape))
    mod_blk = lambda j: pl.BlockSpec((B, 1, D), lambda i: (0, 0, j))
    out_shape = [jax.ShapeDtypeStruct((B, L, A), F32)] * 3 + [jax.ShapeDtypeStruct((B, L, C), F32),
                                                               jax.ShapeDtypeStruct((B, POOL_PRE, C), F32)]
    return pl.pallas_call(
        functools.partial(_inproj_sample_kernel, attn_width=A, pos0=pos0),
        grid=(1,),
        in_specs=[full(B, L, D), mod_blk(0), mod_blk(1), full(1, D), full(*w_pool.shape), full(*w_qkv_t.shape),
                  full(B * L, LANES), full(B * L, LANES), full(*pool_bd.shape), full(1, C), full(B, POOL_PRE, C)],
        out_specs=[full(B, L, A)] * 3 + [full(B, L, C), full(B, POOL_PRE, C)],
        out_shape=out_shape,
        compiler_params=_params(1), name="inproj_sample")(
        x, shift_scale, shift_scale, g.reshape(1, D), w_pool, w_qkv_t, cos_rows, sin_rows, pool_bd,
        pool_scale.reshape(1, C), pre)


def _lambda(lq1_ref, lk1_ref, lq2_ref, lk2_ref, lam_init):
    return (jnp.exp(jnp.sum(lq1_ref[...] * lk1_ref[...], axis=1, keepdims=True))
            - jnp.exp(jnp.sum(lq2_ref[...] * lk2_ref[...], axis=1, keepdims=True)) + lam_init)


def _head_norm(o, g, lam_init):
    return _rms(o, g) * (1.0 - lam_init)


def _prompt_attn_kernel(qi_ref, ki_ref, qt_ref, k_ref, vt_ref, lq1_ref, lk1_ref, lq2_ref, lk2_ref, g_ref,
                        o_ref, qs_ref, m_ref, acc_ref, s_ref, *, tile, lam_init):
    t = pl.program_id(1)
    qi = qi_ref[t]
    ki = ki_ref[t]
    n_heads = qs_ref.shape[0]

    @pl.when(ki == 0)
    def _():
        zeros = jnp.zeros((HEAD_DIM, tile), qs_ref.dtype)
        for h in range(n_heads):
            qs_ref[h, :HEAD_DIM, :tile] = qt_ref[h * HEAD_W:h * HEAD_W + HEAD_DIM, :]
            qs_ref[h, HEAD_DIM:, :tile] = zeros
            qs_ref[h, :HEAD_DIM, tile:] = zeros
            qs_ref[h, HEAD_DIM:, tile:] = qt_ref[h * HEAD_W + HEAD_DIM:(h + 1) * HEAD_W, :]
        m_ref[...] = jnp.full_like(m_ref, NEG_INF)
        acc_ref[...] = jnp.zeros_like(acc_ref)

    def scores(h, masked):
        s = jnp.dot(k_ref[:, h * HEAD_W:(h + 1) * HEAD_W], qs_ref[h], preferred_element_type=F32)
        if masked:
            key = lax.broadcasted_iota(jnp.int32, s.shape, 0)
            qry = lax.broadcasted_iota(jnp.int32, s.shape, 1) % tile
            s = jnp.where(key <= qry, s, NEG_INF)
        s_ref[h % 2] = s
        return jnp.max(s, axis=0, keepdims=True)

    def accumulate(h, m_cur):
        m_prev = m_ref[h]
        m_new = jnp.maximum(m_prev, m_cur)
        alpha = jnp.exp2(m_prev - m_new)
        p = jnp.exp2(s_ref[h % 2] - m_new)
        vt_ext = jnp.concatenate([vt_ref[h * HEAD_W:(h + 1) * HEAD_W, :], jnp.ones((SUM_ROWS, tile), BF16)], axis=0)
        acc_ref[h] = alpha * acc_ref[h] + jnp.dot(vt_ext, p.astype(BF16), preferred_element_type=F32)
        m_ref[h] = m_new

    def step(masked):
        m_cur = [None] * n_heads
        for h in range(n_heads):
            m_cur[h] = scores(h, masked)
            if h > 0:
                accumulate(h - 1, m_cur[h - 1])
        accumulate(n_heads - 1, m_cur[n_heads - 1])

    @pl.when(ki < qi)
    def _():
        step(False)

    @pl.when(ki == qi)
    def _():
        step(True)
        lam = _lambda(lq1_ref, lk1_ref, lq2_ref, lk2_ref, lam_init)
        for h in range(n_heads):
            on = acc_ref[h, :HEAD_W] / acc_ref[h, HEAD_W:HEAD_W + 1]
            o = on[:, :tile] - lam * on[:, tile:]
            y = o * lax.rsqrt(jnp.mean(o * o, axis=0, keepdims=True) + EPS) * g_ref[...] * (1.0 - lam_init)
            o_ref[:, h * HEAD_W:(h + 1) * HEAD_W] = y.T.astype(o_ref.dtype)


def _prompt_attention(qt, k, vt, lams, subln_g, lam_init):
    B, W, S = qt.shape
    tile = min(S, ATTN_TILE)
    nq = S // tile
    n_heads = W // HEAD_W
    qi_tab = jnp.asarray([qi for qi in range(nq) for _ in range(qi + 1)], jnp.int32)
    ki_tab = jnp.asarray([ki for qi in range(nq) for ki in range(qi + 1)], jnp.int32)
    lam_blk = pl.BlockSpec((1, HEAD_DIM), lambda b, t, qt_, kt_: (0, 0))
    grid_spec = pltpu.PrefetchScalarGridSpec(
        num_scalar_prefetch=2,
        grid=(B, int(qi_tab.shape[0])),
        in_specs=[pl.BlockSpec((None, W, tile), lambda b, t, qt_, kt_: (b, 0, qt_[t])),
                  pl.BlockSpec((None, tile, W), lambda b, t, qt_, kt_: (b, kt_[t], 0)),
                  pl.BlockSpec((None, W, tile), lambda b, t, qt_, kt_: (b, 0, kt_[t])),
                  lam_blk, lam_blk, lam_blk, lam_blk,
                  pl.BlockSpec((HEAD_W, 1), lambda b, t, qt_, kt_: (0, 0))],
        out_specs=pl.BlockSpec((None, tile, W), lambda b, t, qt_, kt_: (b, qt_[t], 0)),
        scratch_shapes=[pltpu.VMEM((n_heads, HEAD_W, 2 * tile), BF16),
                        pltpu.VMEM((n_heads, 1, 2 * tile), F32),
                        pltpu.VMEM((n_heads, HEAD_W + SUM_ROWS, 2 * tile), F32),
                        pltpu.VMEM((2, tile, 2 * tile), F32)])
    kern = functools.partial(_prompt_attn_kernel, tile=tile, lam_init=lam_init)
    return pl.pallas_call(kern, grid_spec=grid_spec, out_shape=jax.ShapeDtypeStruct((B, S, W), BF16),
                          compiler_params=_params(2), name="prompt_attention")(
        qi_tab, ki_tab, qt, k, vt, *[x.reshape(1, HEAD_DIM) for x in lams], subln_g.reshape(HEAD_W, 1))


def _key_page_copy(ck_hbm, page, buf_ref, slot, j, sem):
    n = ck_hbm.shape[2]
    return pltpu.make_async_copy(ck_hbm.at[page], buf_ref.at[slot, :, pl.ds(j * n, n)], sem.at[slot])


def _value_page_copy(cv_hbm, page, buf_ref, slot, j, sem):
    n = cv_hbm.shape[1]
    return pltpu.make_async_copy(cv_hbm.at[page], buf_ref.at[slot, pl.ds(j * n, n), :], sem.at[slot])


def _sample_attn_mlp_kernel(pt_ref, q_ref, kn_ref, vn_ref, lq1_ref, lk1_ref, lq2_ref, lk2_ref, g_ref, ck_hbm, cv_hbm,
                            x_ref, yp_ref, op_ref, g1_ref, sh2_ref, sc2_ref, g2_ref, n2_ref, nf_ref, wo_ref, w1_ref,
                            w2_ref, o_ref, y_ref, kbuf_ref, vbuf_ref, s_ref, p_ref, ksem, vsem, x1_ref, h_ref,
                            f_ref, ff_ref, *, n_pages, lam_init, final_norm):
    b = pl.program_id(0)
    n_seq = pl.num_programs(0)
    page_rows = ck_hbm.shape[2]
    chunk = PAGES_PER_CHUNK * page_rows
    n_chunks = n_pages // PAGES_PER_CHUNK
    units = 2 * n_chunks
    past = n_pages * page_rows
    L, W = q_ref.shape
    rows_per_head = 2 * L

    def unit_copies(seq, i, slot):
        c = i % n_chunks
        pages = [pt_ref[seq * n_pages + c * PAGES_PER_CHUNK + j] for j in range(PAGES_PER_CHUNK)]
        if i < n_chunks:
            return [_key_page_copy(ck_hbm, pg, kbuf_ref, slot, j, ksem) for j, pg in enumerate(pages)]
        return [_value_page_copy(cv_hbm, pg, vbuf_ref, slot, j, vsem) for j, pg in enumerate(pages)]

    def start_unit(seq, i):
        for cp in unit_copies(seq, i, i % RING_SLOTS):
            cp.start()

    @pl.when(b == 0)
    def _():
        for i in range(RING_SLOTS - 1):
            start_unit(b, i)

    q = q_ref[...]
    q2 = jnp.concatenate([q] * (2 * N_HEADS), axis=0)
    r_id = lax.broadcasted_iota(jnp.int32, q2.shape, 0) // L
    c_id = lax.broadcasted_iota(jnp.int32, q2.shape, 1) // HEAD_DIM
    q2 = jnp.where(r_id == c_id, q2, 0.0).astype(BF16)
    pad = jnp.zeros((LANES - L, W), F32)
    kn = jnp.concatenate([kn_ref[...], pad], axis=0).astype(BF16)
    vn = jnp.concatenate([vn_ref[...], pad], axis=0).astype(BF16)

    x1, hh = _mlp_front(x_ref[...][None], yp_ref[...], op_ref[...], g1_ref[...], sh2_ref[...], sc2_ref[...],
                        n2_ref[...], wo_ref)
    x1_ref[...] = x1[0]
    h_ref[...] = hh
    n_ff = w1_ref.shape[1] // FF_CHUNK
    mlp_at = {}
    for j in range(2 * n_ff):
        mlp_at.setdefault(j * units // (2 * n_ff), []).append(j)

    acc = [jnp.zeros((2 * rows_per_head, 2 * HEAD_W), F32) for _ in range(N_HEADS // 2)]
    for i in range(units):
        slot = i % RING_SLOTS
        nxt = i + RING_SLOTS - 1
        if nxt < units:
            start_unit(b, nxt)
        else:
            @pl.when(b + 1 < n_seq)
            def _():
                start_unit(b + 1, nxt - units)
        for j in mlp_at.get(i, []):
            c = j // 2
            if j % 2 == 0:
                f_ref[...] = _mlp_up(h_ref[...], w1_ref, c)
            else:
                part = _mlp_down(f_ref[...], w2_ref, c)
                ff_ref[...] = part if c == 0 else ff_ref[...] + part
        for cp in unit_copies(b, i, slot):
            cp.wait()
        if i < n_chunks:
            kt = kbuf_ref[slot].astype(BF16)
            s_ref[:, i * chunk:(i + 1) * chunk] = jnp.dot(q2, kt, preferred_element_type=F32)
            if i == n_chunks - 1:
                sn = lax.dot_general(q2, kn, NT_DIMS, preferred_element_type=F32)
                key = lax.broadcasted_iota(jnp.int32, sn.shape, 1)
                qry = lax.broadcasted_iota(jnp.int32, sn.shape, 0) % L
                s_ref[:, past:] = jnp.where(key <= qry, sn, NEG_INF)
                s = s_ref[...]
                m = jnp.max(s, axis=1, keepdims=True)
                p = jnp.exp2(s - m)
                l = jnp.sum(p, axis=1, keepdims=True)
                p_ref[...] = p.astype(BF16)
        else:
            c = i - n_chunks
            for hp in range(N_HEADS // 2):
                vpair = jnp.concatenate(
                    [vbuf_ref[slot, pl.ds(2 * hp + d, chunk, stride=N_HEADS), :].astype(BF16) for d in range(2)],
                    axis=1)
                ppair = p_ref[2 * hp * rows_per_head:(2 * hp + 2) * rows_per_head, c * chunk:(c + 1) * chunk]
                acc[hp] = acc[hp] + jnp.dot(ppair, vpair, preferred_element_type=F32)
    lam = _lambda(lq1_ref, lk1_ref, lq2_ref, lk2_ref, lam_init)
    outs = []
    for h in range(N_HEADS):
        rows = slice(h * rows_per_head, (h + 1) * rows_per_head)
        d = h % 2
        a = acc[h // 2][d * rows_per_head:(d + 1) * rows_per_head, d * HEAD_W:(d + 1) * HEAD_W]
        a = a + jnp.dot(p_ref[rows, past:], vn[:, h * HEAD_W:(h + 1) * HEAD_W], preferred_element_type=F32)
        on = a / l[rows, :]
        outs.append(_head_norm(on[:L, :] - lam * on[L:, :], g_ref[...], lam_init))
    o_ref[...] = jnp.concatenate(outs, axis=1)
    y_ref[...] = _mlp_back(x1_ref[...], ff_ref[...], g2_ref[...], nf_ref[...], final_norm)


def _sample_attention_prompt_mlp(q, k_new, v_new, ck, cv, page_table, lams, subln_g, lam_init,
                                 x, y_pool, o, mod, norm2_g, final_g, w_out, w1, w2, final_norm):
    B, L, W = q.shape
    PB, S, D = x.shape
    n_pages = page_table.shape[1]
    page_rows = ck.shape[2]
    chunk = PAGES_PER_CHUNK * page_rows
    past = n_pages * page_rows
    n_rows = 2 * N_HEADS * L
    TR = PB * S // B
    assert TR * B == PB * S and S % TR == 0 and TR % 8 == 0
    assert n_pages % PAGES_PER_CHUNK == 0 and (2 * n_pages // PAGES_PER_CHUNK) % RING_SLOTS == 0
    new_blk = pl.BlockSpec((None, L, W), lambda b, pt: (b, 0, 0))
    lam_blk = pl.BlockSpec((1, HEAD_DIM), lambda b, pt: (0, 0))
    row_blk = lambda w: pl.BlockSpec((TR, w), lambda b, pt: (b, 0))
    mod_blk = lambda j: pl.BlockSpec((None, 1, D), lambda b, pt: (b * TR // S, 0, j))
    grid_spec = pltpu.PrefetchScalarGridSpec(
        num_scalar_prefetch=1,
        grid=(B,),
        in_specs=[new_blk, new_blk, new_blk, lam_blk, lam_blk, lam_blk, lam_blk,
                  pl.BlockSpec((1, HEAD_W), lambda b, pt: (0, 0)),
                  pl.BlockSpec(memory_space=pl.ANY), pl.BlockSpec(memory_space=pl.ANY),
                  row_blk(D), row_blk(y_pool.shape[2]), row_blk(o.shape[2]),
                  mod_blk(2), mod_blk(3), mod_blk(4), mod_blk(5),
                  _resident((1, D)), _resident((1, D)), _resident(w_out.shape), _resident(w1.shape),
                  _resident(w2.shape)],
        out_specs=[new_blk, row_blk(D)],
        scratch_shapes=[pltpu.VMEM((RING_SLOTS, W, chunk), ck.dtype),
                        pltpu.VMEM((RING_SLOTS, chunk * N_HEADS, HEAD_W), cv.dtype),
                        pltpu.VMEM((n_rows, past + LANES), F32),
                        pltpu.VMEM((n_rows, past + LANES), BF16),
                        pltpu.SemaphoreType.DMA((RING_SLOTS,)),
                        pltpu.SemaphoreType.DMA((RING_SLOTS,)),
                        pltpu.VMEM((TR, D), F32),
                        pltpu.VMEM((TR, D), BF16),
                        pltpu.VMEM((TR, FF_CHUNK), BF16),
                        pltpu.VMEM((TR, D), F32)])
    kern = functools.partial(_sample_attn_mlp_kernel, n_pages=n_pages, lam_init=lam_init, final_norm=final_norm)
    o_s, y = pl.pallas_call(
        kern, grid_spec=grid_spec,
        out_shape=[jax.ShapeDtypeStruct((B, L, W), F32), jax.ShapeDtypeStruct((PB * S, D), F32)],
        compiler_params=_params(1), name="sample_attention_prompt_mlp")(
        page_table.reshape(-1), q, k_new, v_new, *[t.reshape(1, HEAD_DIM) for t in lams],
        subln_g.reshape(1, HEAD_W), ck, cv,
        x.reshape(PB * S, D), y_pool.reshape(PB * S, -1), o.reshape(PB * S, -1), mod, mod, mod, mod,
        norm2_g.reshape(1, D), final_g.reshape(1, D), w_out, w1, w2)
    return o_s, y.reshape(PB, S, D)


def _mlp_front(x, yp, o, g1, sh2, sc2, n2, wo_ref):
    G, R, D = x.shape
    cat = jnp.concatenate([yp.reshape(G * R, -1).astype(BF16), o.reshape(G * R, -1).astype(BF16)], axis=1)
    mix = jnp.dot(cat, wo_ref[...], preferred_element_type=F32)
    x = x + g1 * mix.reshape(G, R, D)
    h = (_rms(x, n2) * (1.0 + sc2) + sh2).reshape(G * R, D).astype(BF16)
    return x, h


def _mlp_up(h, w1_ref, c):
    f = jnp.dot(h, w1_ref[:, c * FF_CHUNK:(c + 1) * FF_CHUNK], preferred_element_type=F32)
    return jnp.square(jnp.maximum(f, 0.0)).astype(BF16)


def _mlp_down(f, w2_ref, c):
    return jnp.dot(f, w2_ref[c * FF_CHUNK:(c + 1) * FF_CHUNK, :], preferred_element_type=F32)


def _mlp_back(x, acc, g2, nf, final_norm):
    x = x + g2 * acc.reshape(x.shape)
    return _rms(x, nf) if final_norm else x


def _mlp_kernel(x_ref, yp_ref, o_ref, g1_ref, sh2_ref, sc2_ref, g2_ref, n2_ref, nf_ref, wo_ref, w1_ref, w2_ref,
                y_ref, *, final_norm):
    x, h = _mlp_front(x_ref[...], yp_ref[...], o_ref[...], g1_ref[...], sh2_ref[...], sc2_ref[...], n2_ref[...],
                      wo_ref)
    acc = jnp.zeros(h.shape, F32)
    for c in range(w1_ref.shape[1] // FF_CHUNK):
        acc = acc + _mlp_down(_mlp_up(h, w1_ref, c), w2_ref, c)
    y_ref[...] = _mlp_back(x, acc, g2_ref[...], nf_ref[...], final_norm)


def _mlp(x, y_pool, o, mod, norm2_g, final_g, w_out, w1, w2, *, group_rows, final_norm):
    B, L, D = x.shape
    if L >= group_rows:
        G, R = 1, group_rows
    else:
        G, R = group_rows // L, L
    grid = (B // G, L // R)
    blk = lambda w: pl.BlockSpec((G, R, w), lambda b, l: (b, l, 0))
    mod_blk = lambda j: pl.BlockSpec((G, 1, D), lambda b, l: (b, 0, j))
    kern = functools.partial(_mlp_kernel, final_norm=final_norm)
    return pl.pallas_call(
        kern, grid=grid,
        in_specs=[blk(D), blk(y_pool.shape[2]), blk(o.shape[2]), mod_blk(2), mod_blk(3), mod_blk(4), mod_blk(5),
                  _resident((1, D)), _resident((1, D)), _resident(w_out.shape), _resident(w1.shape),
                  _resident(w2.shape)],
        out_specs=blk(D),
        out_shape=jax.ShapeDtypeStruct((B, L, D), F32),
        compiler_params=_params(2), name="mlp")(
        x, y_pool, o, mod, mod, mod, mod, norm2_g.reshape(1, D), final_g.reshape(1, D), w_out, w1, w2)


def _rope_angles(pos):
    inv_freq = ROPE_THETA ** (-jnp.arange(HALF, dtype=F32) * 2.0 / HEAD_DIM)
    ang = pos.astype(F32)[:, None] * inv_freq[None, :]
    return jnp.cos(ang), jnp.sin(ang)


def _rope_row_tables(cos, sin):
    reps = LANES // HEAD_DIM
    return jnp.tile(cos, (1, 2 * reps)), jnp.tile(jnp.concatenate([-sin, sin], axis=1), (1, reps))


def _block_diag(w):
    g, c, d = w.shape
    out = jnp.zeros((g * c, g * d), w.dtype)
    for i in range(g):
        out = out.at[i * c:(i + 1) * c, i * d:(i + 1) * d].set(w[i])
    return out


def kernel(x_prompt, x_sample, c_prompt, c_sample, cache_k, cache_v, state_pool, page_table, w_ada, b_ada, norm1_g, norm2_g, w_in, pool_w, pool_scale, lambda_q1, lambda_k1, lambda_q2, lambda_k2, subln_g, w_out, w_mlp1, w_mlp2, final_g):
    depth = w_ada.shape[0]
    B, S, D = x_prompt.shape
    NB, L, _ = x_sample.shape
    n_pages = page_table.shape[1]
    n_pool_pages, page_rows = cache_k.shape[1], cache_k.shape[2]
    past_len = n_pages * page_rows
    attn_w = cache_k.shape[3] * cache_k.shape[4]
    pool_cols = pool_scale.shape[1]

    cos_p, sin_p = _rope_angles(jnp.arange(S, dtype=jnp.int32))
    cos_s, sin_s = _rope_row_tables(*_rope_angles(past_len + jnp.arange(L, dtype=jnp.int32)))
    cos_s, sin_s = jnp.tile(cos_s, (NB, 1)), jnp.tile(sin_s, (NB, 1))
    c_all = jnp.concatenate([c_prompt, c_sample], axis=0)

    xp, xs = x_prompt, x_sample
    kp_l, vp_l, pp_l, ks_l, vs_l, ps_l = [], [], [], [], [], []
    for l in range(depth):
        lam_init = 0.8 - 0.6 * math.exp(-0.3 * l)
        lams = (lambda_q1[l], lambda_k1[l], lambda_q2[l], lambda_k2[l])
        w_pool = w_in[l][:, :pool_cols].astype(BF16)
        w_qkv_t = w_in[l][:, pool_cols:].T.astype(BF16)
        w_out_b, w1_b, w2_b = w_out[l].astype(BF16), w_mlp1[l].astype(BF16), w_mlp2[l].astype(BF16)
        pool_bd = _block_diag(pool_w[l]).astype(BF16)
        last = l == depth - 1

        mod = _modulation(c_all, w_ada[l], b_ada[l])
        mod_p = mod[:B].reshape(B, 1, N_MOD * D)
        mod_s = mod[B:].reshape(NB, 1, N_MOD * D)

        kt, v, qtb, kb, vtb, yp, tail = _inproj_prompt(xp, mod_p, norm1_g[l], w_pool, w_qkv_t, cos_p.T, sin_p.T,
                                                       pool_bd, pool_scale[l])
        o_p = _prompt_attention(qtb, kb, vtb, lams, subln_g[l], lam_init)
        kp_l.append(kt.reshape(B, 2 * N_HEADS, HEAD_DIM, S).transpose(0, 3, 1, 2))
        vp_l.append(v.reshape(B, S, N_HEADS, HEAD_W))
        pp_l.append(tail[:, POOL_PRE - POOL_BUF:])

        pre = jnp.pad(state_pool[l], ((0, 0), (POOL_PRE - POOL_BUF, 0), (0, 0)))
        k, v, qf, yp_s, tail = _inproj_sample(xs, mod_s, norm1_g[l], w_pool, w_qkv_t, cos_s, sin_s, pool_bd,
                                              pool_scale[l], pre, pos0=past_len)
        ck = cache_k[l].transpose(0, 2, 3, 1).reshape(n_pool_pages, attn_w, page_rows)
        cv = cache_v[l].reshape(n_pool_pages, page_rows * N_HEADS, HEAD_W)
        o_s, xp = _sample_attention_prompt_mlp(qf, k, v, ck, cv, page_table, lams, subln_g[l], lam_init,
                                               xp, yp, o_p, mod_p, norm2_g[l], final_g, w_out_b, w1_b, w2_b, last)
        ks_l.append(k.reshape(NB, L, 2 * N_HEADS, HEAD_DIM))
        vs_l.append(v.reshape(NB, L, N_HEADS, HEAD_W))
        ps_l.append(tail[:, POOL_PRE - POOL_BUF:])
        xs = _mlp(xs, yp_s, o_s, mod_s, norm2_g[l], final_g, w_out_b, w1_b, w2_b, group_rows=TOKEN_TILE,
                  final_norm=last)

    return (xp, xs, jnp.stack(kp_l), jnp.stack(vp_l), jnp.stack(pp_l),
            jnp.stack(ks_l), jnp.stack(vs_l), jnp.stack(ps_l))
```

```python
import functools
import math

import jax
import jax.numpy as jnp
from jax import lax
from jax.experimental import pallas as pl
from jax.experimental.pallas import tpu as pltpu

N_HEADS = 4
HEAD_DIM = 64
HALF = HEAD_DIM // 2
HEAD_W = 2 * HEAD_DIM
POOL_WINDOWS = (2, 4, 8, 16)
POOL_BUF = max(POOL_WINDOWS) - 1
ROPE_THETA = 10000.0
EPS = 1e-6
NEG_INF = -1e30
SCALE = HEAD_DIM ** -0.5
Q_SCALE = SCALE * math.log2(math.e)
N_MOD = 6

LANES = 128
VMEM_LIMIT_BYTES = 56 * 1024 * 1024

POOL_PRE = 16
TOKEN_TILE = 512
ATTN_TILE = 512
KEY_TILES = 2
SUM_ROWS = 16
PAGES_PER_CHUNK = 8
RING_SLOTS = 4
FF_CHUNK = 512

F32 = jnp.float32
BF16 = jnp.bfloat16
NT_DIMS = (((1,), (1,)), ((), ()))


def _rms(x, g):
    return x * lax.rsqrt(jnp.mean(x * x, axis=-1, keepdims=True) + EPS) * g


def _params(n_grid):
    return pltpu.CompilerParams(dimension_semantics=("arbitrary",) * n_grid,
                                vmem_limit_bytes=VMEM_LIMIT_BYTES)


def _resident(shape):
    nd = len(shape)
    return pl.BlockSpec(shape, lambda *_: (0,) * nd, pipeline_mode=pl.Buffered(1))


def _mod_kernel(c_ref, w_ref, b_ref, o_ref):
    c = c_ref[...]
    s = c * jax.nn.sigmoid(c)
    o_ref[...] = jnp.dot(s.astype(BF16), w_ref[...].astype(BF16), preferred_element_type=F32) + b_ref[...]


def _modulation(c, w_ada, b_ada):
    n, d = c.shape
    return pl.pallas_call(
        _mod_kernel,
        grid=(N_MOD,),
        in_specs=[pl.BlockSpec((n, d), lambda j: (0, 0)),
                  pl.BlockSpec((d, d), lambda j: (0, j)),
                  pl.BlockSpec((1, d), lambda j: (0, j))],
        out_specs=pl.BlockSpec((n, d), lambda j: (0, j)),
        out_shape=jax.ShapeDtypeStruct((n, N_MOD * d), F32),
        compiler_params=_params(1),
        name="modulation",
    )(c, w_ada, b_ada.reshape(1, -1))


def _rope_rows(t, cos, sin_signed, first_half):
    outs = []
    for c in range(t.shape[1] // LANES):
        tc = t[:, c * LANES:(c + 1) * LANES]
        rot = jnp.where(first_half, pltpu.roll(tc, LANES - HALF, 1), pltpu.roll(tc, HALF, 1))
        outs.append(tc * cos + rot * sin_signed)
    return jnp.concatenate(outs, axis=1)


def _rope_cols(t, cos, sin):
    outs = []
    for c in range(t.shape[0] // HEAD_DIM):
        x1 = t[c * HEAD_DIM:c * HEAD_DIM + HALF, :]
        x2 = t[c * HEAD_DIM + HALF:(c + 1) * HEAD_DIM, :]
        outs += [x1 * cos - x2 * sin, x2 * cos + x1 * sin]
    return jnp.concatenate(outs, axis=0)


def _pool_mix(u_pool, pre, pw_ref, ps_ref, pos_base, G, R):
    C = u_pool.shape[1]
    u3 = u_pool.reshape(G, R, C)
    ext3 = jnp.concatenate([pre, u3], axis=1)
    tail = ext3[:, R:R + POOL_PRE, :]
    ext = ext3.reshape(G * (POOL_PRE + R), C)
    pos = pos_base + lax.broadcasted_iota(jnp.int32, (G, R, LANES), 1)
    ds = []
    for gi, w in enumerate(POOL_WINDOWS):
        a = ext[:, gi * LANES:(gi + 1) * LANES]
        sft = 1
        while sft < w:
            a = a + pltpu.roll(a, sft, 0)
            sft *= 2
        win = a.reshape(G, POOL_PRE + R, LANES)[:, POOL_PRE:, :]
        cnt = jnp.minimum(w, pos + 1).astype(F32)
        d = win / cnt - u3[:, :, gi * LANES:(gi + 1) * LANES]
        ds.append(d.reshape(G * R, LANES))
    dcat = jnp.concatenate(ds, axis=1)
    y = jnp.dot(dcat.astype(BF16), pw_ref[...], preferred_element_type=F32) * ps_ref[...]
    return y, tail


def _inproj_prompt_kernel(x_ref, shift_ref, scale_ref, g_ref, wp_ref, wqkv_ref, cos_ref, sin_ref, pw_ref, ps_ref,
                          kt_ref, v_ref, qtb_ref, kb_ref, vtb_ref, yp_ref, tail_ref, pre_ref, *, attn_width):
    _, R, _ = x_ref.shape
    A = attn_width
    step = pl.program_id(1)

    @pl.when(step == 0)
    def _():
        pre_ref[...] = jnp.zeros_like(pre_ref)

    h = _rms(x_ref[...], g_ref[...]) * (1.0 + scale_ref[...]) + shift_ref[...]
    h = h.reshape(R, -1).astype(BF16)
    ut = lax.dot_general(wqkv_ref[...], h, NT_DIMS, preferred_element_type=F32)
    cos, sin = cos_ref[...], sin_ref[...]
    qt = _rope_cols(ut[:A], cos, sin) * Q_SCALE
    kt = _rope_cols(ut[A:2 * A], cos, sin)
    vt = ut[2 * A:]
    qtb_ref[...] = qt.reshape(qtb_ref.shape).astype(qtb_ref.dtype)
    kt_ref[...] = kt.reshape(kt_ref.shape)
    vtb_ref[...] = vt.reshape(vtb_ref.shape).astype(vtb_ref.dtype)
    kb_ref[...] = kt.T.reshape(kb_ref.shape).astype(kb_ref.dtype)
    n_heads = A // HEAD_W
    for hd in range(n_heads):
        v_ref[0, pl.ds(hd, R, stride=n_heads), :] = vt[hd * HEAD_W:(hd + 1) * HEAD_W, :].T

    u_pool = jnp.dot(h, wp_ref[...], preferred_element_type=F32)
    y, tail = _pool_mix(u_pool, pre_ref[...], pw_ref, ps_ref, step * R, 1, R)
    yp_ref[...] = y.reshape(yp_ref.shape).astype(yp_ref.dtype)
    tail_ref[...] = tail
    pre_ref[...] = tail


def _inproj_sample_kernel(x_ref, shift_ref, scale_ref, g_ref, wp_ref, wqkv_ref, cos_ref, sin_ref, pw_ref, ps_ref,
                          pre_ref, k_ref, v_ref, q_ref, yp_ref, tail_ref, *, attn_width, pos0):
    G, R, _ = x_ref.shape
    A = attn_width
    h = _rms(x_ref[...], g_ref[...]) * (1.0 + scale_ref[...]) + shift_ref[...]
    h = h.reshape(G * R, -1).astype(BF16)
    u = lax.dot_general(h, wqkv_ref[...], NT_DIMS, preferred_element_type=F32)
    lane = lax.broadcasted_iota(jnp.int32, (G * R, LANES), 1)
    first_half = (lane % HEAD_DIM) < HALF
    cos, sin_signed = cos_ref[...], sin_ref[...]
    q_ref[...] = (_rope_rows(u[:, :A], cos, sin_signed, first_half) * Q_SCALE).reshape(q_ref.shape)
    k_ref[...] = _rope_rows(u[:, A:2 * A], cos, sin_signed, first_half).reshape(k_ref.shape)
    v_ref[...] = u[:, 2 * A:].reshape(v_ref.shape)

    u_pool = jnp.dot(h, wp_ref[...], preferred_element_type=F32)
    y, tail = _pool_mix(u_pool, pre_ref[...], pw_ref, ps_ref, pos0, G, R)
    yp_ref[...] = y.reshape(yp_ref.shape)
    tail_ref[...] = tail


def _inproj_prompt(x, shift_scale, g, w_pool, w_qkv_t, cos_t, sin_t, pool_bd, pool_scale):
    B, S, D = x.shape
    C = pool_bd.shape[0]
    A = w_qkv_t.shape[0] // 3
    R = min(S, TOKEN_TILE)
    rows = lambda w: pl.BlockSpec((1, R, w), lambda b, l: (b, l, 0))
    cols = pl.BlockSpec((1, A, R), lambda b, l: (b, 0, l))
    mod_blk = lambda j: pl.BlockSpec((1, 1, D), lambda b, l: (b, 0, j))
    tab_blk = pl.BlockSpec((HALF, R), lambda b, l: (0, l))
    tail_blk = pl.BlockSpec((1, POOL_PRE, C), lambda b, l: (b, 0, 0))
    n_heads = A // HEAD_W
    v_rows = pl.BlockSpec((1, R * n_heads, HEAD_W), lambda b, l: (b, l, 0))
    out_shape = [jax.ShapeDtypeStruct((B, A, S), F32), jax.ShapeDtypeStruct((B, S * n_heads, HEAD_W), F32),
                 jax.ShapeDtypeStruct((B, A, S), BF16), jax.ShapeDtypeStruct((B, S, A), BF16),
                 jax.ShapeDtypeStruct((B, A, S), BF16), jax.ShapeDtypeStruct((B, S, C), BF16),
                 jax.ShapeDtypeStruct((B, POOL_PRE, C), F32)]
    return pl.pallas_call(
        functools.partial(_inproj_prompt_kernel, attn_width=A),
        grid=(B, S // R),
        in_specs=[rows(D), mod_blk(0), mod_blk(1), _resident((1, D)), _resident(w_pool.shape),
                  _resident(w_qkv_t.shape), tab_blk, tab_blk, _resident(pool_bd.shape), _resident((1, C))],
        out_specs=[cols, v_rows, cols, rows(A), cols, rows(C), tail_blk],
        out_shape=out_shape,
        scratch_shapes=[pltpu.VMEM((1, POOL_PRE, C), F32)],
        compiler_params=_params(2), name="inproj_prompt")(
        x, shift_scale, shift_scale, g.reshape(1, D), w_pool, w_qkv_t, cos_t, sin_t, pool_bd,
        pool_scale.reshape(1, C))


def _inproj_sample(x, shift_scale, g, w_pool, w_qkv_t, cos_rows, sin_rows, pool_bd, pool_scale, pre, *, pos0):
    B, L, D = x.shape
    C = pool_bd.shape[0]
    A = w_qkv_t.shape[0] // 3
    full = lambda *shape: pl.BlockSpec(shape, lambda i: (0,) * len(shape))
    mod_blk = lambda j: pl.BlockSpec((B, 1, D), lambda i: (0, 0, j))
    out_shape = [jax.ShapeDtypeStruct((B, L, A), F32)] * 3 + [jax.ShapeDtypeStruct((B, L, C), F32),
                                                               jax.ShapeDtypeStruct((B, POOL_PRE, C), F32)]
    return pl.pallas_call(
        functools.partial(_inproj_sample_kernel, attn_width=A, pos0=pos0),
        grid=(1,),
        in_specs=[full(B, L, D), mod_blk(0), mod_blk(1), full(1, D), full(*w_pool.shape), full(*w_qkv_t.shape),
                  full(B * L, LANES), full(B * L, LANES), full(*pool_bd.shape), full(1, C), full(B, POOL_PRE, C)],
        out_specs=[full(B, L, A)] * 3 + [full(B, L, C), full(B, POOL_PRE, C)],
        out_shape=out_shape,
        compiler_params=_params(1), name="inproj_sample")(
        x, shift_scale, shift_scale, g.reshape(1, D), w_pool, w_qkv_t, cos_rows, sin_rows, pool_bd,
        pool_scale.reshape(1, C), pre)


def _lambda(lq1_ref, lk1_ref, lq2_ref, lk2_ref, lam_init):
    return (jnp.exp(jnp.sum(lq1_ref[...] * lk1_ref[...], axis=1, keepdims=True))
            - jnp.exp(jnp.sum(lq2_ref[...] * lk2_ref[...], axis=1, keepdims=True)) + lam_init)


def _head_norm(o, g, lam_init):
    return _rms(o, g) * (1.0 - lam_init)


def _prompt_attn_kernel(qi_ref, kp_ref, qt_ref, k_ref, vt_ref, lq1_ref, lk1_ref, lq2_ref, lk2_ref, g_ref,
                        o_ref, qs_ref, m_ref, acc_ref, s_ref, *, tile, lam_init):
    t = pl.program_id(1)
    qi = qi_ref[t]
    kp = kp_ref[t]
    n_heads = qs_ref.shape[0]

    @pl.when(kp == 0)
    def _():
        zeros = jnp.zeros((HEAD_DIM, tile), qs_ref.dtype)
        for h in range(n_heads):
            qs_ref[h, :HEAD_DIM, :tile] = qt_ref[h * HEAD_W:h * HEAD_W + HEAD_DIM, :]
            qs_ref[h, HEAD_DIM:, :tile] = zeros
            qs_ref[h, :HEAD_DIM, tile:] = zeros
            qs_ref[h, HEAD_DIM:, tile:] = qt_ref[h * HEAD_W + HEAD_DIM:(h + 1) * HEAD_W, :]
        m_ref[...] = jnp.full_like(m_ref, NEG_INF)
        acc_ref[...] = jnp.zeros_like(acc_ref)

    def scores(j, sub, h, masked):
        k_h = k_ref[sub * tile:(sub + 1) * tile, h * HEAD_W:(h + 1) * HEAD_W]
        s = jnp.dot(k_h, qs_ref[h], preferred_element_type=F32)
        if masked:
            key = lax.broadcasted_iota(jnp.int32, s.shape, 0)
            qry = lax.broadcasted_iota(jnp.int32, s.shape, 1) % tile
            s = jnp.where(key <= qry, s, NEG_INF)
        s_ref[j % 2] = s
        return jnp.max(s, axis=0, keepdims=True)

    def accumulate(j, sub, h, m_cur):
        m_prev = m_ref[h]
        m_new = jnp.maximum(m_prev, m_cur)
        alpha = jnp.exp2(m_prev - m_new)
        p = jnp.exp2(s_ref[j % 2] - m_new)
        vt_h = vt_ref[h * HEAD_W:(h + 1) * HEAD_W, sub * tile:(sub + 1) * tile]
        vt_ext = jnp.concatenate([vt_h, jnp.ones((SUM_ROWS, tile), BF16)], axis=0)
        acc_ref[h] = alpha * acc_ref[h] + jnp.dot(vt_ext, p.astype(BF16), preferred_element_type=F32)
        m_ref[h] = m_new

    def run(subs):
        chains = [(sub, h, masked) for sub, masked in subs for h in range(n_heads)]
        m_cur = None
        for j, (sub, h, masked) in enumerate(chains):
            m_next = scores(j, sub, h, masked)
            if j > 0:
                accumulate(j - 1, chains[j - 1][0], chains[j - 1][1], m_cur)
            m_cur = m_next
        j = len(chains) - 1
        accumulate(j, chains[j][0], chains[j][1], m_cur)

    def finalize():
        lam = _lambda(lq1_ref, lk1_ref, lq2_ref, lk2_ref, lam_init)
        for h in range(n_heads):
            on = acc_ref[h, :HEAD_W] / acc_ref[h, HEAD_W:HEAD_W + 1]
            o = on[:, :tile] - lam * on[:, tile:]
            y = o * lax.rsqrt(jnp.mean(o * o, axis=0, keepdims=True) + EPS) * g_ref[...] * (1.0 - lam_init)
            o_ref[:, h * HEAD_W:(h + 1) * HEAD_W] = y.T.astype(o_ref.dtype)

    d = qi - kp * KEY_TILES
    for last in range(KEY_TILES):
        @pl.when(d == last)
        def _():
            run([(sub, sub == last) for sub in range(last + 1)])
            finalize()

    @pl.when(d >= KEY_TILES)
    def _():
        run([(sub, False) for sub in range(KEY_TILES)])


def _prompt_attention(qt, k, vt, lams, subln_g, lam_init):
    B, W, S = qt.shape
    tile = min(S, ATTN_TILE)
    nq = S // tile
    assert nq % KEY_TILES == 0
    n_heads = W // HEAD_W
    steps = [(qi, kp) for qi in range(nq) for kp in range(qi // KEY_TILES + 1)]
    qi_tab = jnp.asarray([qi for qi, _ in steps], jnp.int32)
    kp_tab = jnp.asarray([kp for _, kp in steps], jnp.int32)
    lam_blk = pl.BlockSpec((1, HEAD_DIM), lambda b, t, qt_, kt_: (0, 0))
    grid_spec = pltpu.PrefetchScalarGridSpec(
        num_scalar_prefetch=2,
        grid=(B, len(steps)),
        in_specs=[pl.BlockSpec((None, W, tile), lambda b, t, qt_, kt_: (b, 0, qt_[t])),
                  pl.BlockSpec((None, KEY_TILES * tile, W), lambda b, t, qt_, kt_: (b, kt_[t], 0)),
                  pl.BlockSpec((None, W, KEY_TILES * tile), lambda b, t, qt_, kt_: (b, 0, kt_[t])),
                  lam_blk, lam_blk, lam_blk, lam_blk,
                  pl.BlockSpec((HEAD_W, 1), lambda b, t, qt_, kt_: (0, 0))],
        out_specs=pl.BlockSpec((None, tile, W), lambda b, t, qt_, kt_: (b, qt_[t], 0)),
        scratch_shapes=[pltpu.VMEM((n_heads, HEAD_W, 2 * tile), BF16),
                        pltpu.VMEM((n_heads, 1, 2 * tile), F32),
                        pltpu.VMEM((n_heads, HEAD_W + SUM_ROWS, 2 * tile), F32),
                        pltpu.VMEM((2, tile, 2 * tile), F32)])
    kern = functools.partial(_prompt_attn_kernel, tile=tile, lam_init=lam_init)
    return pl.pallas_call(kern, grid_spec=grid_spec, out_shape=jax.ShapeDtypeStruct((B, S, W), BF16),
                          compiler_params=_params(2), name="prompt_attention")(
        qi_tab, kp_tab, qt, k, vt, *[x.reshape(1, HEAD_DIM) for x in lams], subln_g.reshape(HEAD_W, 1))


def _key_page_copy(ck_hbm, page, buf_ref, slot, j, sem):
    n = ck_hbm.shape[2]
    return pltpu.make_async_copy(ck_hbm.at[page], buf_ref.at[slot, :, pl.ds(j * n, n)], sem.at[slot])


def _value_page_copy(cv_hbm, page, buf_ref, slot, j, sem):
    n = cv_hbm.shape[1]
    return pltpu.make_async_copy(cv_hbm.at[page], buf_ref.at[slot, pl.ds(j * n, n), :], sem.at[slot])


def _sample_attn_mlp_kernel(pt_ref, q_ref, kn_ref, vn_ref, lq1_ref, lk1_ref, lq2_ref, lk2_ref, g_ref, ck_hbm, cv_hbm,
                            x_ref, yp_ref, op_ref, g1_ref, sh2_ref, sc2_ref, g2_ref, n2_ref, nf_ref, wo_ref, w1_ref,
                            w2_ref, o_ref, y_ref, kbuf_ref, vbuf_ref, s_ref, p_ref, ksem, vsem, x1_ref, h_ref,
                            f_ref, ff_ref, *, n_pages, lam_init, final_norm):
    b = pl.program_id(0)
    n_seq = pl.num_programs(0)
    page_rows = ck_hbm.shape[2]
    chunk = PAGES_PER_CHUNK * page_rows
    n_chunks = n_pages // PAGES_PER_CHUNK
    units = 2 * n_chunks
    past = n_pages * page_rows
    L, W = q_ref.shape
    rows_per_head = 2 * L

    def unit_copies(seq, i, slot):
        c = i % n_chunks
        pages = [pt_ref[seq * n_pages + c * PAGES_PER_CHUNK + j] for j in range(PAGES_PER_CHUNK)]
        if i < n_chunks:
            return [_key_page_copy(ck_hbm, pg, kbuf_ref, slot, j, ksem) for j, pg in enumerate(pages)]
        return [_value_page_copy(cv_hbm, pg, vbuf_ref, slot, j, vsem) for j, pg in enumerate(pages)]

    def start_unit(seq, i):
        for cp in unit_copies(seq, i, i % RING_SLOTS):
            cp.start()

    @pl.when(b == 0)
    def _():
        for i in range(RING_SLOTS - 1):
            start_unit(b, i)

    q = q_ref[...]
    q2 = jnp.concatenate([q] * (2 * N_HEADS), axis=0)
    r_id = lax.broadcasted_iota(jnp.int32, q2.shape, 0) // L
    c_id = lax.broadcasted_iota(jnp.int32, q2.shape, 1) // HEAD_DIM
    q2 = jnp.where(r_id == c_id, q2, 0.0).astype(BF16)
    pad = jnp.zeros((LANES - L, W), F32)
    kn = jnp.concatenate([kn_ref[...], pad], axis=0).astype(BF16)
    vn = jnp.concatenate([vn_ref[...], pad], axis=0).astype(BF16)

    x1, hh = _mlp_front(x_ref[...][None], yp_ref[...], op_ref[...], g1_ref[...], sh2_ref[...], sc2_ref[...],
                        n2_ref[...], wo_ref)
    x1_ref[...] = x1[0]
    h_ref[...] = hh
    n_ff = w1_ref.shape[1] // FF_CHUNK
    mlp_at = {}
    for j in range(2 * n_ff):
        mlp_at.setdefault(j * units // (2 * n_ff), []).append(j)

    acc = [jnp.zeros((2 * rows_per_head, 2 * HEAD_W), F32) for _ in range(N_HEADS // 2)]
    for i in range(units):
        slot = i % RING_SLOTS
        nxt = i + RING_SLOTS - 1
        if nxt < units:
            start_unit(b, nxt)
        else:
            @pl.when(b + 1 < n_seq)
            def _():
                start_unit(b + 1, nxt - units)
        for j in mlp_at.get(i, []):
            c = j // 2
            if j % 2 == 0:
                f_ref[...] = _mlp_up(h_ref[...], w1_ref, c)
            else:
                part = _mlp_down(f_ref[...], w2_ref, c)
                ff_ref[...] = part if c == 0 else ff_ref[...] + part
        for cp in unit_copies(b, i, slot):
            cp.wait()
        if i < n_chunks:
            kt = kbuf_ref[slot].astype(BF16)
            s_ref[:, i * chunk:(i + 1) * chunk] = jnp.dot(q2, kt, preferred_element_type=F32)
            if i == n_chunks - 1:
                sn = lax.dot_general(q2, kn, NT_DIMS, preferred_element_type=F32)
                key = lax.broadcasted_iota(jnp.int32, sn.shape, 1)
                qry = lax.broadcasted_iota(jnp.int32, sn.shape, 0) % L
                s_ref[:, past:] = jnp.where(key <= qry, sn, NEG_INF)
                s = s_ref[...]
                m = jnp.max(s, axis=1, keepdims=True)
                p = jnp.exp2(s - m)
                l = jnp.sum(p, axis=1, keepdims=True)
                p_ref[...] = p.astype(BF16)
        else:
            c = i - n_chunks
            for hp in range(N_HEADS // 2):
                vpair = jnp.concatenate(
                    [vbuf_ref[slot, pl.ds(2 * hp + d, chunk, stride=N_HEADS), :].astype(BF16) for d in range(2)],
                    axis=1)
                ppair = p_ref[2 * hp * rows_per_head:(2 * hp + 2) * rows_per_head, c * chunk:(c + 1) * chunk]
                acc[hp] = acc[hp] + jnp.dot(ppair, vpair, preferred_element_type=F32)
    lam = _lambda(lq1_ref, lk1_ref, lq2_ref, lk2_ref, lam_init)
    outs = []
    for h in range(N_HEADS):
        rows = slice(h * rows_per_head, (h + 1) * rows_per_head)
        d = h % 2
        a = acc[h // 2][d * rows_per_head:(d + 1) * rows_per_head, d * HEAD_W:(d + 1) * HEAD_W]
        a = a + jnp.dot(p_ref[rows, past:], vn[:, h * HEAD_W:(h + 1) * HEAD_W], preferred_element_type=F32)
        on = a / l[rows, :]
        outs.append(_head_norm(on[:L, :] - lam * on[L:, :], g_ref[...], lam_init))
    o_ref[...] = jnp.concatenate(outs, axis=1)
    y_ref[...] = _mlp_back(x1_ref[...], ff_ref[...], g2_ref[...], nf_ref[...], final_norm)


def _sample_attention_prompt_mlp(q, k_new, v_new, ck, cv, page_table, lams, subln_g, lam_init,
                                 x, y_pool, o, mod, norm2_g, final_g, w_out, w1, w2, final_norm):
    B, L, W = q.shape
    PB, S, D = x.shape
    n_pages = page_table.shape[1]
    page_rows = ck.shape[2]
    chunk = PAGES_PER_CHUNK * page_rows
    past = n_pages * page_rows
    n_rows = 2 * N_HEADS * L
    TR = PB * S // B
    assert TR * B == PB * S and S % TR == 0 and TR % 8 == 0
    assert n_pages % PAGES_PER_CHUNK == 0 and (2 * n_pages // PAGES_PER_CHUNK) % RING_SLOTS == 0
    new_blk = pl.BlockSpec((None, L, W), lambda b, pt: (b, 0, 0))
    lam_blk = pl.BlockSpec((1, HEAD_DIM), lambda b, pt: (0, 0))
    row_blk = lambda w: pl.BlockSpec((TR, w), lambda b, pt: (b, 0))
    mod_blk = lambda j: pl.BlockSpec((None, 1, D), lambda b, pt: (b * TR // S, 0, j))
    grid_spec = pltpu.PrefetchScalarGridSpec(
        num_scalar_prefetch=1,
        grid=(B,),
        in_specs=[new_blk, new_blk, new_blk, lam_blk, lam_blk, lam_blk, lam_blk,
                  pl.BlockSpec((1, HEAD_W), lambda b, pt: (0, 0)),
                  pl.BlockSpec(memory_space=pl.ANY), pl.BlockSpec(memory_space=pl.ANY),
                  row_blk(D), row_blk(y_pool.shape[2]), row_blk(o.shape[2]),
                  mod_blk(2), mod_blk(3), mod_blk(4), mod_blk(5),
                  _resident((1, D)), _resident((1, D)), _resident(w_out.shape), _resident(w1.shape),
                  _resident(w2.shape)],
        out_specs=[new_blk, row_blk(D)],
        scratch_shapes=[pltpu.VMEM((RING_SLOTS, W, chunk), ck.dtype),
                        pltpu.VMEM((RING_SLOTS, chunk * N_HEADS, HEAD_W), cv.dtype),
                        pltpu.VMEM((n_rows, past + LANES), F32),
                        pltpu.VMEM((n_rows, past + LANES), BF16),
                        pltpu.SemaphoreType.DMA((RING_SLOTS,)),
                        pltpu.SemaphoreType.DMA((RING_SLOTS,)),
                        pltpu.VMEM((TR, D), F32),
                        pltpu.VMEM((TR, D), BF16),
                        pltpu.VMEM((TR, FF_CHUNK), BF16),
                        pltpu.VMEM((TR, D), F32)])
    kern = functools.partial(_sample_attn_mlp_kernel, n_pages=n_pages, lam_init=lam_init, final_norm=final_norm)
    o_s, y = pl.pallas_call(
        kern, grid_spec=grid_spec,
        out_shape=[jax.ShapeDtypeStruct((B, L, W), F32), jax.ShapeDtypeStruct((PB * S, D), F32)],
        compiler_params=_params(1), name="sample_attention_prompt_mlp")(
        page_table.reshape(-1), q, k_new, v_new, *[t.reshape(1, HEAD_DIM) for t in lams],
        subln_g.reshape(1, HEAD_W), ck, cv,
        x.reshape(PB * S, D), y_pool.reshape(PB * S, -1), o.reshape(PB * S, -1), mod, mod, mod, mod,
        norm2_g.reshape(1, D), final_g.reshape(1, D), w_out, w1, w2)
    return o_s, y.reshape(PB, S, D)


def _mlp_front(x, yp, o, g1, sh2, sc2, n2, wo_ref):
    G, R, D = x.shape
    cat = jnp.concatenate([yp.reshape(G * R, -1).astype(BF16), o.reshape(G * R, -1).astype(BF16)], axis=1)
    mix = jnp.dot(cat, wo_ref[...], preferred_element_type=F32)
    x = x + g1 * mix.reshape(G, R, D)
    h = (_rms(x, n2) * (1.0 + sc2) + sh2).reshape(G * R, D).astype(BF16)
    return x, h


def _mlp_up(h, w1_ref, c):
    f = jnp.dot(h, w1_ref[:, c * FF_CHUNK:(c + 1) * FF_CHUNK], preferred_element_type=F32)
    return jnp.square(jnp.maximum(f, 0.0)).astype(BF16)


def _mlp_down(f, w2_ref, c):
    return jnp.dot(f, w2_ref[c * FF_CHUNK:(c + 1) * FF_CHUNK, :], preferred_element_type=F32)


def _mlp_back(x, acc, g2, nf, final_norm):
    x = x + g2 * acc.reshape(x.shape)
    return _rms(x, nf) if final_norm else x


def _mlp_kernel(x_ref, yp_ref, o_ref, g1_ref, sh2_ref, sc2_ref, g2_ref, n2_ref, nf_ref, wo_ref, w1_ref, w2_ref,
                y_ref, *, final_norm):
    x, h = _mlp_front(x_ref[...], yp_ref[...], o_ref[...], g1_ref[...], sh2_ref[...], sc2_ref[...], n2_ref[...],
                      wo_ref)
    acc = jnp.zeros(h.shape, F32)
    for c in range(w1_ref.shape[1] // FF_CHUNK):
        acc = acc + _mlp_down(_mlp_up(h, w1_ref, c), w2_ref, c)
    y_ref[...] = _mlp_back(x, acc, g2_ref[...], nf_ref[...], final_norm)


def _mlp(x, y_pool, o, mod, norm2_g, final_g, w_out, w1, w2, *, group_rows, final_norm):
    B, L, D = x.shape
    if L >= group_rows:
        G, R = 1, group_rows
    else:
        G, R = group_rows // L, L
    grid = (B // G, L // R)
    blk = lambda w: pl.BlockSpec((G, R, w), lambda b, l: (b, l, 0))
    mod_blk = lambda j: pl.BlockSpec((G, 1, D), lambda b, l: (b, 0, j))
    kern = functools.partial(_mlp_kernel, final_norm=final_norm)
    return pl.pallas_call(
        kern, grid=grid,
        in_specs=[blk(D), blk(y_pool.shape[2]), blk(o.shape[2]), mod_blk(2), mod_blk(3), mod_blk(4), mod_blk(5),
                  _resident((1, D)), _resident((1, D)), _resident(w_out.shape), _resident(w1.shape),
                  _resident(w2.shape)],
        out_specs=blk(D),
        out_shape=jax.ShapeDtypeStruct((B, L, D), F32),
        compiler_params=_params(2), name="mlp")(
        x, y_pool, o, mod, mod, mod, mod, norm2_g.reshape(1, D), final_g.reshape(1, D), w_out, w1, w2)


def _rope_angles(pos):
    inv_freq = ROPE_THETA ** (-jnp.arange(HALF, dtype=F32) * 2.0 / HEAD_DIM)
    ang = pos.astype(F32)[:, None] * inv_freq[None, :]
    return jnp.cos(ang), jnp.sin(ang)


def _rope_row_tables(cos, sin):
    reps = LANES // HEAD_DIM
    return jnp.tile(cos, (1, 2 * reps)), jnp.tile(jnp.concatenate([-sin, sin], axis=1), (1, reps))


def _block_diag(w):
    g, c, d = w.shape
    out = jnp.zeros((g * c, g * d), w.dtype)
    for i in range(g):
        out = out.at[i * c:(i + 1) * c, i * d:(i + 1) * d].set(w[i])
    return out


def kernel(x_prompt, x_sample, c_prompt, c_sample, cache_k, cache_v, state_pool, page_table, w_ada, b_ada, norm1_g, norm2_g, w_in, pool_w, pool_scale, lambda_q1, lambda_k1, lambda_q2, lambda_k2, subln_g, w_out, w_mlp1, w_mlp2, final_g):
    depth = w_ada.shape[0]
    B, S, D = x_prompt.shape
    NB, L, _ = x_sample.shape
    n_pages = page_table.shape[1]
    n_pool_pages, page_rows = cache_k.shape[1], cache_k.shape[2]
    past_len = n_pages * page_rows
    attn_w = cache_k.shape[3] * cache_k.shape[4]
    pool_cols = pool_scale.shape[1]

    cos_p, sin_p = _rope_angles(jnp.arange(S, dtype=jnp.int32))
    cos_s, sin_s = _rope_row_tables(*_rope_angles(past_len + jnp.arange(L, dtype=jnp.int32)))
    cos_s, sin_s = jnp.tile(cos_s, (NB, 1)), jnp.tile(sin_s, (NB, 1))
    c_all = jnp.concatenate([c_prompt, c_sample], axis=0)

    xp, xs = x_prompt, x_sample
    kp_l, vp_l, pp_l, ks_l, vs_l, ps_l = [], [], [], [], [], []
    for l in range(depth):
        lam_init = 0.8 - 0.6 * math.exp(-0.3 * l)
        lams = (lambda_q1[l], lambda_k1[l], lambda_q2[l], lambda_k2[l])
        w_pool = w_in[l][:, :pool_cols].astype(BF16)
        w_qkv_t = w_in[l][:, pool_cols:].T.astype(BF16)
        w_out_b, w1_b, w2_b = w_out[l].astype(BF16), w_mlp1[l].astype(BF16), w_mlp2[l].astype(BF16)
        pool_bd = _block_diag(pool_w[l]).astype(BF16)
        last = l == depth - 1

        mod = _modulation(c_all, w_ada[l], b_ada[l])
        mod_p = mod[:B].reshape(B, 1, N_MOD * D)
        mod_s = mod[B:].reshape(NB, 1, N_MOD * D)

        kt, v, qtb, kb, vtb, yp, tail = _inproj_prompt(xp, mod_p, norm1_g[l], w_pool, w_qkv_t, cos_p.T, sin_p.T,
                                                       pool_bd, pool_scale[l])
        o_p = _prompt_attention(qtb, kb, vtb, lams, subln_g[l], lam_init)
        kp_l.append(kt.reshape(B, 2 * N_HEADS, HEAD_DIM, S).transpose(0, 3, 1, 2))
        vp_l.append(v.reshape(B, S, N_HEADS, HEAD_W))
        pp_l.append(tail[:, POOL_PRE - POOL_BUF:])

        pre = jnp.pad(state_pool[l], ((0, 0), (POOL_PRE - POOL_BUF, 0), (0, 0)))
        k, v, qf, yp_s, tail = _inproj_sample(xs, mod_s, norm1_g[l], w_pool, w_qkv_t, cos_s, sin_s, pool_bd,
                                              pool_scale[l], pre, pos0=past_len)
        ck = cache_k[l].transpose(0, 2, 3, 1).reshape(n_pool_pages, attn_w, page_rows)
        cv = cache_v[l].reshape(n_pool_pages, page_rows * N_HEADS, HEAD_W)
        o_s, xp = _sample_attention_prompt_mlp(qf, k, v, ck, cv, page_table, lams, subln_g[l], lam_init,
                                               xp, yp, o_p, mod_p, norm2_g[l], final_g, w_out_b, w1_b, w2_b, last)
        ks_l.append(k.reshape(NB, L, 2 * N_HEADS, HEAD_DIM))
        vs_l.append(v.reshape(NB, L, N_HEADS, HEAD_W))
        ps_l.append(tail[:, POOL_PRE - POOL_BUF:])
        xs = _mlp(xs, yp_s, o_s, mod_s, norm2_g[l], final_g, w_out_b, w1_b, w2_b, group_rows=TOKEN_TILE,
                  final_norm=last)

    return (xp, xs, jnp.stack(kp_l), jnp.stack(vp_l), jnp.stack(pp_l),
            jnp.stack(ks_l), jnp.stack(vs_l), jnp.stack(ps_l))
```

```python
import functools
import math

import jax
import jax.numpy as jnp
from jax import lax
from jax.experimental import pallas as pl
from jax.experimental.pallas import tpu as pltpu

N_HEADS = 4
HEAD_DIM = 64
HALF = HEAD_DIM // 2
HEAD_W = 2 * HEAD_DIM
POOL_WINDOWS = (2, 4, 8, 16)
POOL_BUF = max(POOL_WINDOWS) - 1
ROPE_THETA = 10000.0
EPS = 1e-6
NEG_INF = -1e30
SCALE = HEAD_DIM ** -0.5
Q_SCALE = SCALE * math.log2(math.e)
N_MOD = 6

LANES = 128
VMEM_LIMIT_BYTES = 56 * 1024 * 1024

POOL_PRE = 16
TOKEN_TILE = 512
INPROJ_TILE = 1024
ATTN_TILE = 512
KEY_TILES = 4
SUM_ROWS = 16
PAGES_PER_CHUNK = 8
RING_SLOTS = 4
FF_CHUNK = 512

F32 = jnp.float32
BF16 = jnp.bfloat16
NT_DIMS = (((1,), (1,)), ((), ()))


def _rms(x, g):
    return x * lax.rsqrt(jnp.mean(x * x, axis=-1, keepdims=True) + EPS) * g


def _params(n_grid):
    return pltpu.CompilerParams(dimension_semantics=("arbitrary",) * n_grid,
                                vmem_limit_bytes=VMEM_LIMIT_BYTES)


def _resident(shape):
    nd = len(shape)
    return pl.BlockSpec(shape, lambda *_: (0,) * nd, pipeline_mode=pl.Buffered(1))


def _mod_kernel(c_ref, w_ref, b_ref, o_ref):
    c = c_ref[...]
    s = c * jax.nn.sigmoid(c)
    o_ref[...] = jnp.dot(s.astype(BF16), w_ref[...].astype(BF16), preferred_element_type=F32) + b_ref[...]


def _modulation(c, w_ada, b_ada):
    n, d = c.shape
    return pl.pallas_call(
        _mod_kernel,
        grid=(N_MOD,),
        in_specs=[pl.BlockSpec((n, d), lambda j: (0, 0)),
                  pl.BlockSpec((d, d), lambda j: (0, j)),
                  pl.BlockSpec((1, d), lambda j: (0, j))],
        out_specs=pl.BlockSpec((n, d), lambda j: (0, j)),
        out_shape=jax.ShapeDtypeStruct((n, N_MOD * d), F32),
        compiler_params=_params(1),
        name="modulation",
    )(c, w_ada, b_ada.reshape(1, -1))


def _rope_rows(t, cos, sin_signed, first_half):
    outs = []
    for c in range(t.shape[1] // LANES):
        tc = t[:, c * LANES:(c + 1) * LANES]
        rot = jnp.where(first_half, pltpu.roll(tc, LANES - HALF, 1), pltpu.roll(tc, HALF, 1))
        outs.append(tc * cos + rot * sin_signed)
    return jnp.concatenate(outs, axis=1)


def _rope_cols(t, cos, sin):
    outs = []
    for c in range(t.shape[0] // HEAD_DIM):
        x1 = t[c * HEAD_DIM:c * HEAD_DIM + HALF, :]
        x2 = t[c * HEAD_DIM + HALF:(c + 1) * HEAD_DIM, :]
        outs += [x1 * cos - x2 * sin, x2 * cos + x1 * sin]
    return jnp.concatenate(outs, axis=0)


def _pool_mix(u_pool, pre, pw_ref, ps_ref, pos_base, G, R):
    C = u_pool.shape[1]
    u3 = u_pool.reshape(G, R, C)
    ext3 = jnp.concatenate([pre, u3], axis=1)
    tail = ext3[:, R:R + POOL_PRE, :]
    ext = ext3.reshape(G * (POOL_PRE + R), C)
    pos = pos_base + lax.broadcasted_iota(jnp.int32, (G, R, LANES), 1)
    ds = []
    for gi, w in enumerate(POOL_WINDOWS):
        a = ext[:, gi * LANES:(gi + 1) * LANES]
        sft = 1
        while sft < w:
            a = a + pltpu.roll(a, sft, 0)
            sft *= 2
        win = a.reshape(G, POOL_PRE + R, LANES)[:, POOL_PRE:, :]
        cnt = jnp.minimum(w, pos + 1).astype(F32)
        d = win / cnt - u3[:, :, gi * LANES:(gi + 1) * LANES]
        ds.append(d.reshape(G * R, LANES))
    dcat = jnp.concatenate(ds, axis=1)
    y = jnp.dot(dcat.astype(BF16), pw_ref[...], preferred_element_type=F32) * ps_ref[...]
    return y, tail


def _inproj_prompt_kernel(x_ref, shift_ref, scale_ref, g_ref, wp_ref, wqkv_ref, cos_ref, sin_ref, pw_ref, ps_ref,
                          kt_ref, v_ref, qtb_ref, kb_ref, vtb_ref, yp_ref, tail_ref, pre_ref, *, attn_width):
    _, R, _ = x_ref.shape
    A = attn_width
    step = pl.program_id(1)

    @pl.when(step == 0)
    def _():
        pre_ref[...] = jnp.zeros_like(pre_ref)

    h = _rms(x_ref[...], g_ref[...]) * (1.0 + scale_ref[...]) + shift_ref[...]
    h = h.reshape(R, -1).astype(BF16)
    ut = lax.dot_general(wqkv_ref[...], h, NT_DIMS, preferred_element_type=F32)
    cos, sin = cos_ref[...], sin_ref[...]
    qt = _rope_cols(ut[:A], cos, sin) * Q_SCALE
    kt = _rope_cols(ut[A:2 * A], cos, sin)
    vt = ut[2 * A:]
    qtb_ref[...] = qt.reshape(qtb_ref.shape).astype(qtb_ref.dtype)
    kt_ref[...] = kt.reshape(kt_ref.shape)
    vtb_ref[...] = vt.reshape(vtb_ref.shape).astype(vtb_ref.dtype)
    kb_ref[...] = kt.T.reshape(kb_ref.shape).astype(kb_ref.dtype)
    n_heads = A // HEAD_W
    for hd in range(n_heads):
        v_ref[0, pl.ds(hd, R, stride=n_heads), :] = vt[hd * HEAD_W:(hd + 1) * HEAD_W, :].T

    u_pool = jnp.dot(h, wp_ref[...], preferred_element_type=F32)
    y, tail = _pool_mix(u_pool, pre_ref[...], pw_ref, ps_ref, step * R, 1, R)
    yp_ref[...] = y.reshape(yp_ref.shape).astype(yp_ref.dtype)
    tail_ref[...] = tail
    pre_ref[...] = tail


def _inproj_sample_kernel(x_ref, shift_ref, scale_ref, g_ref, wp_ref, wqkv_ref, cos_ref, sin_ref, pw_ref, ps_ref,
                          pre_ref, k_ref, v_ref, q_ref, yp_ref, tail_ref, *, attn_width, pos0):
    G, R, _ = x_ref.shape
    A = attn_width
    h = _rms(x_ref[...], g_ref[...]) * (1.0 + scale_ref[...]) + shift_ref[...]
    h = h.reshape(G * R, -1).astype(BF16)
    u = lax.dot_general(h, wqkv_ref[...], NT_DIMS, preferred_element_type=F32)
    lane = lax.broadcasted_iota(jnp.int32, (G * R, LANES), 1)
    first_half = (lane % HEAD_DIM) < HALF
    cos, sin_signed = cos_ref[...], sin_ref[...]
    q_ref[...] = (_rope_rows(u[:, :A], cos, sin_signed, first_half) * Q_SCALE).reshape(q_ref.shape)
    k_ref[...] = _rope_rows(u[:, A:2 * A], cos, sin_signed, first_half).reshape(k_ref.shape)
    v_ref[...] = u[:, 2 * A:].reshape(v_ref.shape)

    u_pool = jnp.dot(h, wp_ref[...], preferred_element_type=F32)
    y, tail = _pool_mix(u_pool, pre_ref[...], pw_ref, ps_ref, pos0, G, R)
    yp_ref[...] = y.reshape(yp_ref.shape)
    tail_ref[...] = tail


def _inproj_prompt(x, shift_scale, g, w_pool, w_qkv_t, cos_t, sin_t, pool_bd, pool_scale):
    B, S, D = x.shape
    C = pool_bd.shape[0]
    A = w_qkv_t.shape[0] // 3
    R = min(S, INPROJ_TILE)
    rows = lambda w: pl.BlockSpec((1, R, w), lambda b, l: (b, l, 0))
    cols = pl.BlockSpec((1, A, R), lambda b, l: (b, 0, l))
    mod_blk = lambda j: pl.BlockSpec((1, 1, D), lambda b, l: (b, 0, j))
    tab_blk = pl.BlockSpec((HALF, R), lambda b, l: (0, l))
    tail_blk = pl.BlockSpec((1, POOL_PRE, C), lambda b, l: (b, 0, 0))
    n_heads = A // HEAD_W
    v_rows = pl.BlockSpec((1, R * n_heads, HEAD_W), lambda b, l: (b, l, 0))
    out_shape = [jax.ShapeDtypeStruct((B, A, S), F32), jax.ShapeDtypeStruct((B, S * n_heads, HEAD_W), F32),
                 jax.ShapeDtypeStruct((B, A, S), BF16), jax.ShapeDtypeStruct((B, S, A), BF16),
                 jax.ShapeDtypeStruct((B, A, S), BF16), jax.ShapeDtypeStruct((B, S, C), BF16),
                 jax.ShapeDtypeStruct((B, POOL_PRE, C), F32)]
    return pl.pallas_call(
        functools.partial(_inproj_prompt_kernel, attn_width=A),
        grid=(B, S // R),
        in_specs=[rows(D), mod_blk(0), mod_blk(1), _resident((1, D)), _resident(w_pool.shape),
                  _resident(w_qkv_t.shape), tab_blk, tab_blk, _resident(pool_bd.shape), _resident((1, C))],
        out_specs=[cols, v_rows, cols, rows(A), cols, rows(C), tail_blk],
        out_shape=out_shape,
        scratch_shapes=[pltpu.VMEM((1, POOL_PRE, C), F32)],
        compiler_params=_params(2), name="inproj_prompt")(
        x, shift_scale, shift_scale, g.reshape(1, D), w_pool, w_qkv_t, cos_t, sin_t, pool_bd,
        pool_scale.reshape(1, C))


def _inproj_sample(x, shift_scale, g, w_pool, w_qkv_t, cos_rows, sin_rows, pool_bd, pool_scale, pre, *, pos0):
    B, L, D = x.shape
    C = pool_bd.shape[0]
    A = w_qkv_t.shape[0] // 3
    full = lambda *shape: pl.BlockSpec(shape, lambda i: (0,) * len(shape))
    mod_blk = lambda j: pl.BlockSpec((B, 1, D), lambda i: (0, 0, j))
    out_shape = [jax.ShapeDtypeStruct((B, L, A), F32)] * 3 + [jax.ShapeDtypeStruct((B, L, C), F32),
                                                               jax.ShapeDtypeStruct((B, POOL_PRE, C), F32)]
    return pl.pallas_call(
        functools.partial(_inproj_sample_kernel, attn_width=A, pos0=pos0),
        grid=(1,),
        in_specs=[full(B, L, D), mod_blk(0), mod_blk(1), full(1, D), full(*w_pool.shape), full(*w_qkv_t.shape),
                  full(B * L, LANES), full(B * L, LANES), full(*pool_bd.shape), full(1, C), full(B, POOL_PRE, C)],
        out_specs=[full(B, L, A)] * 3 + [full(B, L, C), full(B, POOL_PRE, C)],
        out_shape=out_shape,
        compiler_params=_params(1), name="inproj_sample")(
        x, shift_scale, shift_scale, g.reshape(1, D), w_pool, w_qkv_t, cos_rows, sin_rows, pool_bd,
        pool_scale.reshape(1, C), pre)


def _lambda(lq1_ref, lk1_ref, lq2_ref, lk2_ref, lam_init):
    return (jnp.exp(jnp.sum(lq1_ref[...] * lk1_ref[...], axis=1, keepdims=True))
            - jnp.exp(jnp.sum(lq2_ref[...] * lk2_ref[...], axis=1, keepdims=True)) + lam_init)


def _head_norm(o, g, lam_init):
    return _rms(o, g) * (1.0 - lam_init)


def _prompt_attn_kernel(qi_ref, kp_ref, qt_ref, k_ref, vt_ref, lq1_ref, lk1_ref, lq2_ref, lk2_ref, g_ref,
                        o_ref, qs_ref, m_ref, acc_ref, s_ref, *, tile, lam_init):
    t = pl.program_id(1)
    qi = qi_ref[t]
    kp = kp_ref[t]
    n_heads = qs_ref.shape[0]

    @pl.when(kp == 0)
    def _():
        zeros = jnp.zeros((HEAD_DIM, tile), qs_ref.dtype)
        for h in range(n_heads):
            qs_ref[h, :HEAD_DIM, :tile] = qt_ref[h * HEAD_W:h * HEAD_W + HEAD_DIM, :]
            qs_ref[h, HEAD_DIM:, :tile] = zeros
            qs_ref[h, :HEAD_DIM, tile:] = zeros
            qs_ref[h, HEAD_DIM:, tile:] = qt_ref[h * HEAD_W + HEAD_DIM:(h + 1) * HEAD_W, :]
        m_ref[...] = jnp.full_like(m_ref, NEG_INF)
        acc_ref[...] = jnp.zeros_like(acc_ref)

    def scores(j, sub, h, masked):
        k_h = k_ref[sub * tile:(sub + 1) * tile, h * HEAD_W:(h + 1) * HEAD_W]
        s = jnp.dot(k_h, qs_ref[h], preferred_element_type=F32)
        if masked:
            key = lax.broadcasted_iota(jnp.int32, s.shape, 0)
            qry = lax.broadcasted_iota(jnp.int32, s.shape, 1) % tile
            s = jnp.where(key <= qry, s, NEG_INF)
        s_ref[j % 2] = s
        return jnp.max(s, axis=0, keepdims=True)

    def accumulate(j, sub, h, m_cur):
        m_prev = m_ref[h]
        m_new = jnp.maximum(m_prev, m_cur)
        alpha = jnp.exp2(m_prev - m_new)
        p = jnp.exp2(s_ref[j % 2] - m_new)
        vt_h = vt_ref[h * HEAD_W:(h + 1) * HEAD_W, sub * tile:(sub + 1) * tile]
        vt_ext = jnp.concatenate([vt_h, jnp.ones((SUM_ROWS, tile), BF16)], axis=0)
        acc_ref[h] = alpha * acc_ref[h] + jnp.dot(vt_ext, p.astype(BF16), preferred_element_type=F32)
        m_ref[h] = m_new

    def run(subs):
        chains = [(sub, h, masked) for sub, masked in subs for h in range(n_heads)]
        m_cur = None
        for j, (sub, h, masked) in enumerate(chains):
            m_next = scores(j, sub, h, masked)
            if j > 0:
                accumulate(j - 1, chains[j - 1][0], chains[j - 1][1], m_cur)
            m_cur = m_next
        j = len(chains) - 1
        accumulate(j, chains[j][0], chains[j][1], m_cur)

    def finalize():
        lam = _lambda(lq1_ref, lk1_ref, lq2_ref, lk2_ref, lam_init)
        for h in range(n_heads):
            on = acc_ref[h, :HEAD_W] / acc_ref[h, HEAD_W:HEAD_W + 1]
            o = on[:, :tile] - lam * on[:, tile:]
            y = o * lax.rsqrt(jnp.mean(o * o, axis=0, keepdims=True) + EPS) * g_ref[...] * (1.0 - lam_init)
            o_ref[:, h * HEAD_W:(h + 1) * HEAD_W] = y.T.astype(o_ref.dtype)

    d = qi - kp * KEY_TILES
    for last in range(KEY_TILES):
        @pl.when(d == last)
        def _():
            run([(sub, sub == last) for sub in range(last + 1)])
            finalize()

    @pl.when(d >= KEY_TILES)
    def _():
        run([(sub, False) for sub in range(KEY_TILES)])


def _prompt_attention(qt, k, vt, lams, subln_g, lam_init):
    B, W, S = qt.shape
    tile = min(S, ATTN_TILE)
    nq = S // tile
    assert nq % KEY_TILES == 0
    n_heads = W // HEAD_W
    steps = [(qi, kp) for qi in range(nq) for kp in range(qi // KEY_TILES + 1)]
    qi_tab = jnp.asarray([qi for qi, _ in steps], jnp.int32)
    kp_tab = jnp.asarray([kp for _, kp in steps], jnp.int32)
    lam_blk = pl.BlockSpec((1, HEAD_DIM), lambda b, t, qt_, kt_: (0, 0))
    grid_spec = pltpu.PrefetchScalarGridSpec(
        num_scalar_prefetch=2,
        grid=(B, len(steps)),
        in_specs=[pl.BlockSpec((None, W, tile), lambda b, t, qt_, kt_: (b, 0, qt_[t])),
                  pl.BlockSpec((None, KEY_TILES * tile, W), lambda b, t, qt_, kt_: (b, kt_[t], 0)),
                  pl.BlockSpec((None, W, KEY_TILES * tile), lambda b, t, qt_, kt_: (b, 0, kt_[t])),
                  lam_blk, lam_blk, lam_blk, lam_blk,
                  pl.BlockSpec((HEAD_W, 1), lambda b, t, qt_, kt_: (0, 0))],
        out_specs=pl.BlockSpec((None, tile, W), lambda b, t, qt_, kt_: (b, qt_[t], 0)),
        scratch_shapes=[pltpu.VMEM((n_heads, HEAD_W, 2 * tile), BF16),
                        pltpu.VMEM((n_heads, 1, 2 * tile), F32),
                        pltpu.VMEM((n_heads, HEAD_W + SUM_ROWS, 2 * tile), F32),
                        pltpu.VMEM((2, tile, 2 * tile), F32)])
    kern = functools.partial(_prompt_attn_kernel, tile=tile, lam_init=lam_init)
    return pl.pallas_call(kern, grid_spec=grid_spec, out_shape=jax.ShapeDtypeStruct((B, S, W), BF16),
                          compiler_params=_params(2), name="prompt_attention")(
        qi_tab, kp_tab, qt, k, vt, *[x.reshape(1, HEAD_DIM) for x in lams], subln_g.reshape(HEAD_W, 1))


def _key_page_copy(ck_hbm, page, buf_ref, slot, j, sem):
    n = ck_hbm.shape[2]
    return pltpu.make_async_copy(ck_hbm.at[page], buf_ref.at[slot, :, pl.ds(j * n, n)], sem.at[slot])


def _value_page_copy(cv_hbm, page, buf_ref, slot, j, sem):
    n = cv_hbm.shape[1]
    return pltpu.make_async_copy(cv_hbm.at[page], buf_ref.at[slot, pl.ds(j * n, n), :], sem.at[slot])


def _sample_attn_mlp_kernel(pt_ref, q_ref, kn_ref, vn_ref, lq1_ref, lk1_ref, lq2_ref, lk2_ref, g_ref, ck_hbm, cv_hbm,
                            x_ref, yp_ref, op_ref, g1_ref, sh2_ref, sc2_ref, g2_ref, n2_ref, nf_ref, wo_ref, w1_ref,
                            w2_ref, o_ref, y_ref, kbuf_ref, vbuf_ref, s_ref, p_ref, ksem, vsem, x1_ref, h_ref,
                            f_ref, ff_ref, *, n_pages, lam_init, final_norm):
    b = pl.program_id(0)
    n_seq = pl.num_programs(0)
    page_rows = ck_hbm.shape[2]
    chunk = PAGES_PER_CHUNK * page_rows
    n_chunks = n_pages // PAGES_PER_CHUNK
    units = 2 * n_chunks
    past = n_pages * page_rows
    L, W = q_ref.shape
    rows_per_head = 2 * L

    def unit_copies(seq, i, slot):
        c = i % n_chunks
        pages = [pt_ref[seq * n_pages + c * PAGES_PER_CHUNK + j] for j in range(PAGES_PER_CHUNK)]
        if i < n_chunks:
            return [_key_page_copy(ck_hbm, pg, kbuf_ref, slot, j, ksem) for j, pg in enumerate(pages)]
        return [_value_page_copy(cv_hbm, pg, vbuf_ref, slot, j, vsem) for j, pg in enumerate(pages)]

    def start_unit(seq, i):
        for cp in unit_copies(seq, i, i % RING_SLOTS):
            cp.start()

    @pl.when(b == 0)
    def _():
        for i in range(RING_SLOTS - 1):
            start_unit(b, i)

    q = q_ref[...]
    q2 = jnp.concatenate([q] * (2 * N_HEADS), axis=0)
    r_id = lax.broadcasted_iota(jnp.int32, q2.shape, 0) // L
    c_id = lax.broadcasted_iota(jnp.int32, q2.shape, 1) // HEAD_DIM
    q2 = jnp.where(r_id == c_id, q2, 0.0).astype(BF16)
    pad = jnp.zeros((LANES - L, W), F32)
    kn = jnp.concatenate([kn_ref[...], pad], axis=0).astype(BF16)
    vn = jnp.concatenate([vn_ref[...], pad], axis=0).astype(BF16)

    x1, hh = _mlp_front(x_ref[...][None], yp_ref[...], op_ref[...], g1_ref[...], sh2_ref[...], sc2_ref[...],
                        n2_ref[...], wo_ref)
    x1_ref[...] = x1[0]
    h_ref[...] = hh
    n_ff = w1_ref.shape[1] // FF_CHUNK
    mlp_at = {}
    for j in range(2 * n_ff):
        mlp_at.setdefault(j * units // (2 * n_ff), []).append(j)

    acc = [jnp.zeros((2 * rows_per_head, 2 * HEAD_W), F32) for _ in range(N_HEADS // 2)]
    for i in range(units):
        slot = i % RING_SLOTS
        nxt = i + RING_SLOTS - 1
        if nxt < units:
            start_unit(b, nxt)
        else:
            @pl.when(b + 1 < n_seq)
            def _():
                start_unit(b + 1, nxt - units)
        for j in mlp_at.get(i, []):
            c = j // 2
            if j % 2 == 0:
                f_ref[...] = _mlp_up(h_ref[...], w1_ref, c)
            else:
                part = _mlp_down(f_ref[...], w2_ref, c)
                ff_ref[...] = part if c == 0 else ff_ref[...] + part
        for cp in unit_copies(b, i, slot):
            cp.wait()
        if i < n_chunks:
            kt = kbuf_ref[slot].astype(BF16)
            s_ref[:, i * chunk:(i + 1) * chunk] = jnp.dot(q2, kt, preferred_element_type=F32)
            if i == n_chunks - 1:
                sn = lax.dot_general(q2, kn, NT_DIMS, preferred_element_type=F32)
                key = lax.broadcasted_iota(jnp.int32, sn.shape, 1)
                qry = lax.broadcasted_iota(jnp.int32, sn.shape, 0) % L
                s_ref[:, past:] = jnp.where(key <= qry, sn, NEG_INF)
                s = s_ref[...]
                m = jnp.max(s, axis=1, keepdims=True)
                p = jnp.exp2(s - m)
                l = jnp.sum(p, axis=1, keepdims=True)
                p_ref[...] = p.astype(BF16)
        else:
            c = i - n_chunks
            for hp in range(N_HEADS // 2):
                vpair = jnp.concatenate(
                    [vbuf_ref[slot, pl.ds(2 * hp + d, chunk, stride=N_HEADS), :].astype(BF16) for d in range(2)],
                    axis=1)
                ppair = p_ref[2 * hp * rows_per_head:(2 * hp + 2) * rows_per_head, c * chunk:(c + 1) * chunk]
                acc[hp] = acc[hp] + jnp.dot(ppair, vpair, preferred_element_type=F32)
    lam = _lambda(lq1_ref, lk1_ref, lq2_ref, lk2_ref, lam_init)
    outs = []
    for h in range(N_HEADS):
        rows = slice(h * rows_per_head, (h + 1) * rows_per_head)
        d = h % 2
        a = acc[h // 2][d * rows_per_head:(d + 1) * rows_per_head, d * HEAD_W:(d + 1) * HEAD_W]
        a = a + jnp.dot(p_ref[rows, past:], vn[:, h * HEAD_W:(h + 1) * HEAD_W], preferred_element_type=F32)
        on = a / l[rows, :]
        outs.append(_head_norm(on[:L, :] - lam * on[L:, :], g_ref[...], lam_init))
    o_ref[...] = jnp.concatenate(outs, axis=1)
    y_ref[...] = _mlp_back(x1_ref[...], ff_ref[...], g2_ref[...], nf_ref[...], final_norm)


def _sample_attention_prompt_mlp(q, k_new, v_new, ck, cv, page_table, lams, subln_g, lam_init,
                                 x, y_pool, o, mod, norm2_g, final_g, w_out, w1, w2, final_norm):
    B, L, W = q.shape
    PB, S, D = x.shape
    n_pages = page_table.shape[1]
    page_rows = ck.shape[2]
    chunk = PAGES_PER_CHUNK * page_rows
    past = n_pages * page_rows
    n_rows = 2 * N_HEADS * L
    TR = PB * S // B
    assert TR * B == PB * S and S % TR == 0 and TR % 8 == 0
    assert n_pages % PAGES_PER_CHUNK == 0 and (2 * n_pages // PAGES_PER_CHUNK) % RING_SLOTS == 0
    new_blk = pl.BlockSpec((None, L, W), lambda b, pt: (b, 0, 0))
    lam_blk = pl.BlockSpec((1, HEAD_DIM), lambda b, pt: (0, 0))
    row_blk = lambda w: pl.BlockSpec((TR, w), lambda b, pt: (b, 0))
    mod_blk = lambda j: pl.BlockSpec((None, 1, D), lambda b, pt: (b * TR // S, 0, j))
    grid_spec = pltpu.PrefetchScalarGridSpec(
        num_scalar_prefetch=1,
        grid=(B,),
        in_specs=[new_blk, new_blk, new_blk, lam_blk, lam_blk, lam_blk, lam_blk,
                  pl.BlockSpec((1, HEAD_W), lambda b, pt: (0, 0)),
                  pl.BlockSpec(memory_space=pl.ANY), pl.BlockSpec(memory_space=pl.ANY),
                  row_blk(D), row_blk(y_pool.shape[2]), row_blk(o.shape[2]),
                  mod_blk(2), mod_blk(3), mod_blk(4), mod_blk(5),
                  _resident((1, D)), _resident((1, D)), _resident(w_out.shape), _resident(w1.shape),
                  _resident(w2.shape)],
        out_specs=[new_blk, row_blk(D)],
        scratch_shapes=[pltpu.VMEM((RING_SLOTS, W, chunk), ck.dtype),
                        pltpu.VMEM((RING_SLOTS, chunk * N_HEADS, HEAD_W), cv.dtype),
                        pltpu.VMEM((n_rows, past + LANES), F32),
                        pltpu.VMEM((n_rows, past + LANES), BF16),
                        pltpu.SemaphoreType.DMA((RING_SLOTS,)),
                        pltpu.SemaphoreType.DMA((RING_SLOTS,)),
                        pltpu.VMEM((TR, D), F32),
                        pltpu.VMEM((TR, D), BF16),
                        pltpu.VMEM((TR, FF_CHUNK), BF16),
                        pltpu.VMEM((TR, D), F32)])
    kern = functools.partial(_sample_attn_mlp_kernel, n_pages=n_pages, lam_init=lam_init, final_norm=final_norm)
    o_s, y = pl.pallas_call(
        kern, grid_spec=grid_spec,
        out_shape=[jax.ShapeDtypeStruct((B, L, W), F32), jax.ShapeDtypeStruct((PB * S, D), F32)],
        compiler_params=_params(1), name="sample_attention_prompt_mlp")(
        page_table.reshape(-1), q, k_new, v_new, *[t.reshape(1, HEAD_DIM) for t in lams],
        subln_g.reshape(1, HEAD_W), ck, cv,
        x.reshape(PB * S, D), y_pool.reshape(PB * S, -1), o.reshape(PB * S, -1), mod, mod, mod, mod,
        norm2_g.reshape(1, D), final_g.reshape(1, D), w_out, w1, w2)
    return o_s, y.reshape(PB, S, D)


def _mlp_front(x, yp, o, g1, sh2, sc2, n2, wo_ref):
    G, R, D = x.shape
    cat = jnp.concatenate([yp.reshape(G * R, -1).astype(BF16), o.reshape(G * R, -1).astype(BF16)], axis=1)
    mix = jnp.dot(cat, wo_ref[...], preferred_element_type=F32)
    x = x + g1 * mix.reshape(G, R, D)
    h = (_rms(x, n2) * (1.0 + sc2) + sh2).reshape(G * R, D).astype(BF16)
    return x, h


def _mlp_up(h, w1_ref, c):
    f = jnp.dot(h, w1_ref[:, c * FF_CHUNK:(c + 1) * FF_CHUNK], preferred_element_type=F32)
    return jnp.square(jnp.maximum(f, 0.0)).astype(BF16)


def _mlp_down(f, w2_ref, c):
    return jnp.dot(f, w2_ref[c * FF_CHUNK:(c + 1) * FF_CHUNK, :], preferred_element_type=F32)


def _mlp_back(x, acc, g2, nf, final_norm):
    x = x + g2 * acc.reshape(x.shape)
    return _rms(x, nf) if final_norm else x


def _mlp_kernel(x_ref, yp_ref, o_ref, g1_ref, sh2_ref, sc2_ref, g2_ref, n2_ref, nf_ref, wo_ref, w1_ref, w2_ref,
                y_ref, *, final_norm):
    x, h = _mlp_front(x_ref[...], yp_ref[...], o_ref[...], g1_ref[...], sh2_ref[...], sc2_ref[...], n2_ref[...],
                      wo_ref)
    acc = jnp.zeros(h.shape, F32)
    for c in range(w1_ref.shape[1] // FF_CHUNK):
        acc = acc + _mlp_down(_mlp_up(h, w1_ref, c), w2_ref, c)
    y_ref[...] = _mlp_back(x, acc, g2_ref[...], nf_ref[...], final_norm)


def _mlp(x, y_pool, o, mod, norm2_g, final_g, w_out, w1, w2, *, group_rows, final_norm):
    B, L, D = x.shape
    if L >= group_rows:
        G, R = 1, group_rows
    else:
        G, R = group_rows // L, L
    grid = (B // G, L // R)
    blk = lambda w: pl.BlockSpec((G, R, w), lambda b, l: (b, l, 0))
    mod_blk = lambda j: pl.BlockSpec((G, 1, D), lambda b, l: (b, 0, j))
    kern = functools.partial(_mlp_kernel, final_norm=final_norm)
    return pl.pallas_call(
        kern, grid=grid,
        in_specs=[blk(D), blk(y_pool.shape[2]), blk(o.shape[2]), mod_blk(2), mod_blk(3), mod_blk(4), mod_blk(5),
                  _resident((1, D)), _resident((1, D)), _resident(w_out.shape), _resident(w1.shape),
                  _resident(w2.shape)],
        out_specs=blk(D),
        out_shape=jax.ShapeDtypeStruct((B, L, D), F32),
        compiler_params=_params(2), name="mlp")(
        x, y_pool, o, mod, mod, mod, mod, norm2_g.reshape(1, D), final_g.reshape(1, D), w_out, w1, w2)


def _rope_angles(pos):
    inv_freq = ROPE_THETA ** (-jnp.arange(HALF, dtype=F32) * 2.0 / HEAD_DIM)
    ang = pos.astype(F32)[:, None] * inv_freq[None, :]
    return jnp.cos(ang), jnp.sin(ang)


def _rope_row_tables(cos, sin):
    reps = LANES // HEAD_DIM
    return jnp.tile(cos, (1, 2 * reps)), jnp.tile(jnp.concatenate([-sin, sin], axis=1), (1, reps))


def _block_diag(w):
    g, c, d = w.shape
    out = jnp.zeros((g * c, g * d), w.dtype)
    for i in range(g):
        out = out.at[i * c:(i + 1) * c, i * d:(i + 1) * d].set(w[i])
    return out


def kernel(x_prompt, x_sample, c_prompt, c_sample, cache_k, cache_v, state_pool, page_table, w_ada, b_ada, norm1_g, norm2_g, w_in, pool_w, pool_scale, lambda_q1, lambda_k1, lambda_q2, lambda_k2, subln_g, w_out, w_mlp1, w_mlp2, final_g):
    depth = w_ada.shape[0]
    B, S, D = x_prompt.shape
    NB, L, _ = x_sample.shape
    n_pages = page_table.shape[1]
    n_pool_pages, page_rows = cache_k.shape[1], cache_k.shape[2]
    past_len = n_pages * page_rows
    attn_w = cache_k.shape[3] * cache_k.shape[4]
    pool_cols = pool_scale.shape[1]

    cos_p, sin_p = _rope_angles(jnp.arange(S, dtype=jnp.int32))
    cos_s, sin_s = _rope_row_tables(*_rope_angles(past_len + jnp.arange(L, dtype=jnp.int32)))
    cos_s, sin_s = jnp.tile(cos_s, (NB, 1)), jnp.tile(sin_s, (NB, 1))
    c_all = jnp.concatenate([c_prompt, c_sample], axis=0)

    xp, xs = x_prompt, x_sample
    kp_l, vp_l, pp_l, ks_l, vs_l, ps_l = [], [], [], [], [], []
    for l in range(depth):
        lam_init = 0.8 - 0.6 * math.exp(-0.3 * l)
        lams = (lambda_q1[l], lambda_k1[l], lambda_q2[l], lambda_k2[l])
        w_pool = w_in[l][:, :pool_cols].astype(BF16)
        w_qkv_t = w_in[l][:, pool_cols:].T.astype(BF16)
        w_out_b, w1_b, w2_b = w_out[l].astype(BF16), w_mlp1[l].astype(BF16), w_mlp2[l].astype(BF16)
        pool_bd = _block_diag(pool_w[l]).astype(BF16)
        last = l == depth - 1

        mod = _modulation(c_all, w_ada[l], b_ada[l])
        mod_p = mod[:B].reshape(B, 1, N_MOD * D)
        mod_s = mod[B:].reshape(NB, 1, N_MOD * D)

        kt, v, qtb, kb, vtb, yp, tail = _inproj_prompt(xp, mod_p, norm1_g[l], w_pool, w_qkv_t, cos_p.T, sin_p.T,
                                                       pool_bd, pool_scale[l])
        o_p = _prompt_attention(qtb, kb, vtb, lams, subln_g[l], lam_init)
        kp_l.append(kt.reshape(B, 2 * N_HEADS, HEAD_DIM, S).transpose(0, 3, 1, 2))
        vp_l.append(v.reshape(B, S, N_HEADS, HEAD_W))
        pp_l.append(tail[:, POOL_PRE - POOL_BUF:])

        pre = jnp.pad(state_pool[l], ((0, 0), (POOL_PRE - POOL_BUF, 0), (0, 0)))
        k, v, qf, yp_s, tail = _inproj_sample(xs, mod_s, norm1_g[l], w_pool, w_qkv_t, cos_s, sin_s, pool_bd,
                                              pool_scale[l], pre, pos0=past_len)
        ck = cache_k[l].transpose(0, 2, 3, 1).reshape(n_pool_pages, attn_w, page_rows)
        cv = cache_v[l].reshape(n_pool_pages, page_rows * N_HEADS, HEAD_W)
        o_s, xp = _sample_attention_prompt_mlp(qf, k, v, ck, cv, page_table, lams, subln_g[l], lam_init,
                                               xp, yp, o_p, mod_p, norm2_g[l], final_g, w_out_b, w1_b, w2_b, last)
        ks_l.append(k.reshape(NB, L, 2 * N_HEADS, HEAD_DIM))
        vs_l.append(v.reshape(NB, L, N_HEADS, HEAD_W))
        ps_l.append(tail[:, POOL_PRE - POOL_BUF:])
        xs = _mlp(xs, yp_s, o_s, mod_s, norm2_g[l], final_g, w_out_b, w1_b, w2_b, group_rows=TOKEN_TILE,
                  final_norm=last)

    return (xp, xs, jnp.stack(kp_l), jnp.stack(vp_l), jnp.stack(pp_l),
            jnp.stack(ks_l), jnp.stack(vs_l), jnp.stack(ps_l))
```

```python
import functools
import math

import jax
import jax.numpy as jnp
from jax import lax
from jax.experimental import pallas as pl
from jax.experimental.pallas import tpu as pltpu

N_HEADS = 4
HEAD_DIM = 64
HALF = HEAD_DIM // 2
HEAD_W = 2 * HEAD_DIM
POOL_WINDOWS = (2, 4, 8, 16)
POOL_BUF = max(POOL_WINDOWS) - 1
ROPE_THETA = 10000.0
EPS = 1e-6
NEG_INF = -1e30
SCALE = HEAD_DIM ** -0.5
Q_SCALE = SCALE * math.log2(math.e)
N_MOD = 6

LANES = 128
VMEM_LIMIT_BYTES = 56 * 1024 * 1024

POOL_PRE = 16
TOKEN_TILE = 512
INPROJ_TILE = 1024
ATTN_TILE = 512
KEY_TILES = 2
SUM_ROWS = 16
PAGES_PER_CHUNK = 8
RING_SLOTS = 4
CACHE_DMA_PRIORITY = 1
FF_CHUNK = 512

F32 = jnp.float32
BF16 = jnp.bfloat16
NT_DIMS = (((1,), (1,)), ((), ()))


def _rms(x, g):
    return x * lax.rsqrt(jnp.mean(x * x, axis=-1, keepdims=True) + EPS) * g


def _params(n_grid):
    return pltpu.CompilerParams(dimension_semantics=("arbitrary",) * n_grid,
                                vmem_limit_bytes=VMEM_LIMIT_BYTES)


def _resident(shape):
    nd = len(shape)
    return pl.BlockSpec(shape, lambda *_: (0,) * nd, pipeline_mode=pl.Buffered(1))


def _mod_kernel(c_ref, w_ref, b_ref, o_ref):
    c = c_ref[...]
    s = c * jax.nn.sigmoid(c)
    o_ref[...] = jnp.dot(s.astype(BF16), w_ref[...].astype(BF16), preferred_element_type=F32) + b_ref[...]


def _modulation(c, w_ada, b_ada):
    n, d = c.shape
    return pl.pallas_call(
        _mod_kernel,
        grid=(N_MOD,),
        in_specs=[pl.BlockSpec((n, d), lambda j: (0, 0)),
                  pl.BlockSpec((d, d), lambda j: (0, j)),
                  pl.BlockSpec((1, d), lambda j: (0, j))],
        out_specs=pl.BlockSpec((n, d), lambda j: (0, j)),
        out_shape=jax.ShapeDtypeStruct((n, N_MOD * d), F32),
        compiler_params=_params(1),
        name="modulation",
    )(c, w_ada, b_ada.reshape(1, -1))


def _rope_rows(t, cos, sin_signed, first_half):
    outs = []
    for c in range(t.shape[1] // LANES):
        tc = t[:, c * LANES:(c + 1) * LANES]
        rot = jnp.where(first_half, pltpu.roll(tc, LANES - HALF, 1), pltpu.roll(tc, HALF, 1))
        outs.append(tc * cos + rot * sin_signed)
    return jnp.concatenate(outs, axis=1)


def _rope_cols(t, cos, sin):
    outs = []
    for c in range(t.shape[0] // HEAD_DIM):
        x1 = t[c * HEAD_DIM:c * HEAD_DIM + HALF, :]
        x2 = t[c * HEAD_DIM + HALF:(c + 1) * HEAD_DIM, :]
        outs += [x1 * cos - x2 * sin, x2 * cos + x1 * sin]
    return jnp.concatenate(outs, axis=0)


def _pool_mix(u_pool, pre, pw_ref, ps_ref, pos_base, G, R):
    C = u_pool.shape[1]
    u3 = u_pool.reshape(G, R, C)
    ext3 = jnp.concatenate([pre, u3], axis=1)
    tail = ext3[:, R:R + POOL_PRE, :]
    ext = ext3.reshape(G * (POOL_PRE + R), C)
    pos = pos_base + lax.broadcasted_iota(jnp.int32, (G, R, LANES), 1)
    ds = []
    for gi, w in enumerate(POOL_WINDOWS):
        a = ext[:, gi * LANES:(gi + 1) * LANES]
        sft = 1
        while sft < w:
            a = a + pltpu.roll(a, sft, 0)
            sft *= 2
        win = a.reshape(G, POOL_PRE + R, LANES)[:, POOL_PRE:, :]
        cnt = jnp.minimum(w, pos + 1).astype(F32)
        d = win / cnt - u3[:, :, gi * LANES:(gi + 1) * LANES]
        ds.append(d.reshape(G * R, LANES))
    dcat = jnp.concatenate(ds, axis=1)
    y = jnp.dot(dcat.astype(BF16), pw_ref[...], preferred_element_type=F32) * ps_ref[...]
    return y, tail


def _inproj_prompt_kernel(x_ref, shift_ref, scale_ref, g_ref, wp_ref, wqkv_ref, cos_ref, sin_ref, pw_ref, ps_ref,
                          kt_ref, v_ref, qtb_ref, kb_ref, vtb_ref, yp_ref, tail_ref, pre_ref, *, attn_width):
    _, R, _ = x_ref.shape
    A = attn_width
    step = pl.program_id(1)

    @pl.when(step == 0)
    def _():
        pre_ref[...] = jnp.zeros_like(pre_ref)

    h = _rms(x_ref[...], g_ref[...]) * (1.0 + scale_ref[...]) + shift_ref[...]
    h = h.reshape(R, -1).astype(BF16)
    ut = lax.dot_general(wqkv_ref[...], h, NT_DIMS, preferred_element_type=F32)
    cos, sin = cos_ref[...], sin_ref[...]
    qt = _rope_cols(ut[:A], cos, sin) * Q_SCALE
    kt = _rope_cols(ut[A:2 * A], cos, sin)
    vt = ut[2 * A:]
    qtb_ref[...] = qt.reshape(qtb_ref.shape).astype(qtb_ref.dtype)
    kt_ref[...] = kt.reshape(kt_ref.shape)
    vtb_ref[...] = vt.reshape(vtb_ref.shape).astype(vtb_ref.dtype)
    kb_ref[...] = kt.T.reshape(kb_ref.shape).astype(kb_ref.dtype)
    n_heads = A // HEAD_W
    for hd in range(n_heads):
        v_ref[0, pl.ds(hd, R, stride=n_heads), :] = vt[hd * HEAD_W:(hd + 1) * HEAD_W, :].T

    u_pool = jnp.dot(h, wp_ref[...], preferred_element_type=F32)
    y, tail = _pool_mix(u_pool, pre_ref[...], pw_ref, ps_ref, step * R, 1, R)
    yp_ref[...] = y.reshape(yp_ref.shape).astype(yp_ref.dtype)
    tail_ref[...] = tail
    pre_ref[...] = tail


def _inproj_sample_kernel(x_ref, shift_ref, scale_ref, g_ref, wp_ref, wqkv_ref, cos_ref, sin_ref, pw_ref, ps_ref,
                          pre_ref, k_ref, v_ref, q_ref, yp_ref, tail_ref, *, attn_width, pos0):
    G, R, _ = x_ref.shape
    A = attn_width
    h = _rms(x_ref[...], g_ref[...]) * (1.0 + scale_ref[...]) + shift_ref[...]
    h = h.reshape(G * R, -1).astype(BF16)
    u = lax.dot_general(h, wqkv_ref[...], NT_DIMS, preferred_element_type=F32)
    lane = lax.broadcasted_iota(jnp.int32, (G * R, LANES), 1)
    first_half = (lane % HEAD_DIM) < HALF
    cos, sin_signed = cos_ref[...], sin_ref[...]
    q_ref[...] = (_rope_rows(u[:, :A], cos, sin_signed, first_half) * Q_SCALE).reshape(q_ref.shape)
    k_ref[...] = _rope_rows(u[:, A:2 * A], cos, sin_signed, first_half).reshape(k_ref.shape)
    v_ref[...] = u[:, 2 * A:].reshape(v_ref.shape)

    u_pool = jnp.dot(h, wp_ref[...], preferred_element_type=F32)
    y, tail = _pool_mix(u_pool, pre_ref[...], pw_ref, ps_ref, pos0, G, R)
    yp_ref[...] = y.reshape(yp_ref.shape)
    tail_ref[...] = tail


def _inproj_prompt(x, shift_scale, g, w_pool, w_qkv_t, cos_t, sin_t, pool_bd, pool_scale):
    B, S, D = x.shape
    C = pool_bd.shape[0]
    A = w_qkv_t.shape[0] // 3
    R = min(S, INPROJ_TILE)
    rows = lambda w: pl.BlockSpec((1, R, w), lambda b, l: (b, l, 0))
    cols = pl.BlockSpec((1, A, R), lambda b, l: (b, 0, l))
    mod_blk = lambda j: pl.BlockSpec((1, 1, D), lambda b, l: (b, 0, j))
    tab_blk = pl.BlockSpec((HALF, R), lambda b, l: (0, l))
    tail_blk = pl.BlockSpec((1, POOL_PRE, C), lambda b, l: (b, 0, 0))
    n_heads = A // HEAD_W
    v_rows = pl.BlockSpec((1, R * n_heads, HEAD_W), lambda b, l: (b, l, 0))
    out_shape = [jax.ShapeDtypeStruct((B, A, S), F32), jax.ShapeDtypeStruct((B, S * n_heads, HEAD_W), F32),
                 jax.ShapeDtypeStruct((B, A, S), BF16), jax.ShapeDtypeStruct((B, S, A), BF16),
                 jax.ShapeDtypeStruct((B, A, S), BF16), jax.ShapeDtypeStruct((B, S, C), BF16),
                 jax.ShapeDtypeStruct((B, POOL_PRE, C), F32)]
    return pl.pallas_call(
        functools.partial(_inproj_prompt_kernel, attn_width=A),
        grid=(B, S // R),
        in_specs=[rows(D), mod_blk(0), mod_blk(1), _resident((1, D)), _resident(w_pool.shape),
                  _resident(w_qkv_t.shape), tab_blk, tab_blk, _resident(pool_bd.shape), _resident((1, C))],
        out_specs=[cols, v_rows, cols, rows(A), cols, rows(C), tail_blk],
        out_shape=out_shape,
        scratch_shapes=[pltpu.VMEM((1, POOL_PRE, C), F32)],
        compiler_params=_params(2), name="inproj_prompt")(
        x, shift_scale, shift_scale, g.reshape(1, D), w_pool, w_qkv_t, cos_t, sin_t, pool_bd,
        pool_scale.reshape(1, C))


def _inproj_sample(x, shift_scale, g, w_pool, w_qkv_t, cos_rows, sin_rows, pool_bd, pool_scale, pre, *, pos0):
    B, L, D = x.shape
    C = pool_bd.shape[0]
    A = w_qkv_t.shape[0] // 3
    full = lambda *shape: pl.BlockSpec(shape, lambda i: (0,) * len(shape))
    mod_blk = lambda j: pl.BlockSpec((B, 1, D), lambda i: (0, 0, j))
    out_shape = [jax.ShapeDtypeStruct((B, L, A), F32)] * 3 + [jax.ShapeDtypeStruct((B, L, C), F32),
                                                               jax.ShapeDtypeStruct((B, POOL_PRE, C), F32)]
    return pl.pallas_call(
        functools.partial(_inproj_sample_kernel, attn_width=A, pos0=pos0),
        grid=(1,),
        in_specs=[full(B, L, D), mod_blk(0), mod_blk(1), full(1, D), full(*w_pool.shape), full(*w_qkv_t.shape),
                  full(B * L, LANES), full(B * L, LANES), full(*pool_bd.shape), full(1, C), full(B, POOL_PRE, C)],
        out_specs=[full(B, L, A)] * 3 + [full(B, L, C), full(B, POOL_PRE, C)],
        out_shape=out_shape,
        compiler_params=_params(1), name="inproj_sample")(
        x, shift_scale, shift_scale, g.reshape(1, D), w_pool, w_qkv_t, cos_rows, sin_rows, pool_bd,
        pool_scale.reshape(1, C), pre)


def _lambda(lq1_ref, lk1_ref, lq2_ref, lk2_ref, lam_init):
    return (jnp.exp(jnp.sum(lq1_ref[...] * lk1_ref[...], axis=1, keepdims=True))
            - jnp.exp(jnp.sum(lq2_ref[...] * lk2_ref[...], axis=1, keepdims=True)) + lam_init)


def _head_norm(o, g, lam_init):
    return _rms(o, g) * (1.0 - lam_init)


def _prompt_attn_kernel(qi_ref, kp_ref, qt_ref, k_ref, vt_ref, lq1_ref, lk1_ref, lq2_ref, lk2_ref, g_ref,
                        o_ref, qs_ref, m_ref, acc_ref, s_ref, *, tile, lam_init):
    t = pl.program_id(1)
    qi = qi_ref[t]
    kp = kp_ref[t]
    n_heads = qs_ref.shape[0]

    @pl.when(kp == 0)
    def _():
        zeros = jnp.zeros((HEAD_DIM, tile), qs_ref.dtype)
        for h in range(n_heads):
            qs_ref[h, :HEAD_DIM, :tile] = qt_ref[h * HEAD_W:h * HEAD_W + HEAD_DIM, :]
            qs_ref[h, HEAD_DIM:, :tile] = zeros
            qs_ref[h, :HEAD_DIM, tile:] = zeros
            qs_ref[h, HEAD_DIM:, tile:] = qt_ref[h * HEAD_W + HEAD_DIM:(h + 1) * HEAD_W, :]
        m_ref[...] = jnp.full_like(m_ref, NEG_INF)
        acc_ref[...] = jnp.zeros_like(acc_ref)

    def scores(j, sub, h, masked):
        k_h = k_ref[sub * tile:(sub + 1) * tile, h * HEAD_W:(h + 1) * HEAD_W]
        s = jnp.dot(k_h, qs_ref[h], preferred_element_type=F32)
        if masked:
            key = lax.broadcasted_iota(jnp.int32, s.shape, 0)
            qry = lax.broadcasted_iota(jnp.int32, s.shape, 1) % tile
            s = jnp.where(key <= qry, s, NEG_INF)
        s_ref[j % 2] = s
        return jnp.max(s, axis=0, keepdims=True)

    def accumulate(j, sub, h, m_cur):
        m_prev = m_ref[h]
        m_new = jnp.maximum(m_prev, m_cur)
        alpha = jnp.exp2(m_prev - m_new)
        p = jnp.exp2(s_ref[j % 2] - m_new)
        vt_h = vt_ref[h * HEAD_W:(h + 1) * HEAD_W, sub * tile:(sub + 1) * tile]
        vt_ext = jnp.concatenate([vt_h, jnp.ones((SUM_ROWS, tile), BF16)], axis=0)
        acc_ref[h] = alpha * acc_ref[h] + jnp.dot(vt_ext, p.astype(BF16), preferred_element_type=F32)
        m_ref[h] = m_new

    def run(subs):
        chains = [(sub, h, masked) for sub, masked in subs for h in range(n_heads)]
        m_cur = None
        for j, (sub, h, masked) in enumerate(chains):
            m_next = scores(j, sub, h, masked)
            if j > 0:
                accumulate(j - 1, chains[j - 1][0], chains[j - 1][1], m_cur)
            m_cur = m_next
        j = len(chains) - 1
        accumulate(j, chains[j][0], chains[j][1], m_cur)

    def finalize():
        lam = _lambda(lq1_ref, lk1_ref, lq2_ref, lk2_ref, lam_init)
        for h in range(n_heads):
            on = acc_ref[h, :HEAD_W] / acc_ref[h, HEAD_W:HEAD_W + 1]
            o = on[:, :tile] - lam * on[:, tile:]
            y = o * lax.rsqrt(jnp.mean(o * o, axis=0, keepdims=True) + EPS) * g_ref[...] * (1.0 - lam_init)
            o_ref[:, h * HEAD_W:(h + 1) * HEAD_W] = y.T.astype(o_ref.dtype)

    d = qi - kp * KEY_TILES
    for last in range(KEY_TILES):
        @pl.when(d == last)
        def _():
            run([(sub, sub == last) for sub in range(last + 1)])
            finalize()

    @pl.when(d >= KEY_TILES)
    def _():
        run([(sub, False) for sub in range(KEY_TILES)])


def _prompt_attention(qt, k, vt, lams, subln_g, lam_init):
    B, W, S = qt.shape
    tile = min(S, ATTN_TILE)
    nq = S // tile
    assert nq % KEY_TILES == 0
    n_heads = W // HEAD_W
    steps = [(qi, kp) for qi in range(nq) for kp in range(qi // KEY_TILES + 1)]
    qi_tab = jnp.asarray([qi for qi, _ in steps], jnp.int32)
    kp_tab = jnp.asarray([kp for _, kp in steps], jnp.int32)
    lam_blk = pl.BlockSpec((1, HEAD_DIM), lambda b, t, qt_, kt_: (0, 0))
    grid_spec = pltpu.PrefetchScalarGridSpec(
        num_scalar_prefetch=2,
        grid=(B, len(steps)),
        in_specs=[pl.BlockSpec((None, W, tile), lambda b, t, qt_, kt_: (b, 0, qt_[t])),
                  pl.BlockSpec((None, KEY_TILES * tile, W), lambda b, t, qt_, kt_: (b, kt_[t], 0)),
                  pl.BlockSpec((None, W, KEY_TILES * tile), lambda b, t, qt_, kt_: (b, 0, kt_[t])),
                  lam_blk, lam_blk, lam_blk, lam_blk,
                  pl.BlockSpec((HEAD_W, 1), lambda b, t, qt_, kt_: (0, 0))],
        out_specs=pl.BlockSpec((None, tile, W), lambda b, t, qt_, kt_: (b, qt_[t], 0)),
        scratch_shapes=[pltpu.VMEM((n_heads, HEAD_W, 2 * tile), BF16),
                        pltpu.VMEM((n_heads, 1, 2 * tile), F32),
                        pltpu.VMEM((n_heads, HEAD_W + SUM_ROWS, 2 * tile), F32),
                        pltpu.VMEM((2, tile, 2 * tile), F32)])
    kern = functools.partial(_prompt_attn_kernel, tile=tile, lam_init=lam_init)
    return pl.pallas_call(kern, grid_spec=grid_spec, out_shape=jax.ShapeDtypeStruct((B, S, W), BF16),
                          compiler_params=_params(2), name="prompt_attention")(
        qi_tab, kp_tab, qt, k, vt, *[x.reshape(1, HEAD_DIM) for x in lams], subln_g.reshape(HEAD_W, 1))


def _key_page_copy(ck_hbm, page, buf_ref, slot, j, sem):
    n = ck_hbm.shape[2]
    return pltpu.make_async_copy(ck_hbm.at[page], buf_ref.at[slot, :, pl.ds(j * n, n)], sem.at[slot])


def _value_page_copy(cv_hbm, page, buf_ref, slot, j, sem):
    n = cv_hbm.shape[1]
    return pltpu.make_async_copy(cv_hbm.at[page], buf_ref.at[slot, pl.ds(j * n, n), :], sem.at[slot])


def _sample_attn_mlp_kernel(pt_ref, q_ref, kn_ref, vn_ref, lq1_ref, lk1_ref, lq2_ref, lk2_ref, g_ref, ck_hbm, cv_hbm,
                            x_ref, yp_ref, op_ref, g1_ref, sh2_ref, sc2_ref, g2_ref, n2_ref, nf_ref, wo_ref, w1_ref,
                            w2_ref, o_ref, y_ref, kbuf_ref, vbuf_ref, s_ref, p_ref, ksem, vsem, x1_ref, h_ref,
                            f_ref, ff_ref, *, n_pages, lam_init, final_norm):
    b = pl.program_id(0)
    n_seq = pl.num_programs(0)
    page_rows = ck_hbm.shape[2]
    chunk = PAGES_PER_CHUNK * page_rows
    n_chunks = n_pages // PAGES_PER_CHUNK
    units = 2 * n_chunks
    past = n_pages * page_rows
    L, W = q_ref.shape
    rows_per_head = 2 * L

    def unit_copies(seq, i, slot):
        c = i % n_chunks
        pages = [pt_ref[seq * n_pages + c * PAGES_PER_CHUNK + j] for j in range(PAGES_PER_CHUNK)]
        if i < n_chunks:
            return [_key_page_copy(ck_hbm, pg, kbuf_ref, slot, j, ksem) for j, pg in enumerate(pages)]
        return [_value_page_copy(cv_hbm, pg, vbuf_ref, slot, j, vsem) for j, pg in enumerate(pages)]

    def start_unit(seq, i):
        for cp in unit_copies(seq, i, i % RING_SLOTS):
            cp.start(priority=CACHE_DMA_PRIORITY)

    @pl.when(b == 0)
    def _():
        for i in range(RING_SLOTS - 1):
            start_unit(b, i)

    q = q_ref[...]
    q2 = jnp.concatenate([q] * (2 * N_HEADS), axis=0)
    r_id = lax.broadcasted_iota(jnp.int32, q2.shape, 0) // L
    c_id = lax.broadcasted_iota(jnp.int32, q2.shape, 1) // HEAD_DIM
    q2 = jnp.where(r_id == c_id, q2, 0.0).astype(BF16)
    pad = jnp.zeros((LANES - L, W), F32)
    kn = jnp.concatenate([kn_ref[...], pad], axis=0).astype(BF16)
    vn = jnp.concatenate([vn_ref[...], pad], axis=0).astype(BF16)

    x1, hh = _mlp_front(x_ref[...][None], yp_ref[...], op_ref[...], g1_ref[...], sh2_ref[...], sc2_ref[...],
                        n2_ref[...], wo_ref)
    x1_ref[...] = x1[0]
    h_ref[...] = hh
    n_ff = w1_ref.shape[1] // FF_CHUNK
    mlp_at = {}
    for j in range(2 * n_ff):
        mlp_at.setdefault(j * units // (2 * n_ff), []).append(j)

    acc = [jnp.zeros((2 * rows_per_head, 2 * HEAD_W), F32) for _ in range(N_HEADS // 2)]
    for i in range(units):
        slot = i % RING_SLOTS
        nxt = i + RING_SLOTS - 1
        if nxt < units:
            start_unit(b, nxt)
        else:
            @pl.when(b + 1 < n_seq)
            def _():
                start_unit(b + 1, nxt - units)
        for j in mlp_at.get(i, []):
            c = j // 2
            if j % 2 == 0:
                f_ref[...] = _mlp_up(h_ref[...], w1_ref, c)
            else:
                part = _mlp_down(f_ref[...], w2_ref, c)
                ff_ref[...] = part if c == 0 else ff_ref[...] + part
        for cp in unit_copies(b, i, slot):
            cp.wait()
        if i < n_chunks:
            kt = kbuf_ref[slot].astype(BF16)
            s_ref[:, i * chunk:(i + 1) * chunk] = jnp.dot(q2, kt, preferred_element_type=F32)
            if i == n_chunks - 1:
                sn = lax.dot_general(q2, kn, NT_DIMS, preferred_element_type=F32)
                key = lax.broadcasted_iota(jnp.int32, sn.shape, 1)
                qry = lax.broadcasted_iota(jnp.int32, sn.shape, 0) % L
                s_ref[:, past:] = jnp.where(key <= qry, sn, NEG_INF)
                s = s_ref[...]
                m = jnp.max(s, axis=1, keepdims=True)
                p = jnp.exp2(s - m)
                l = jnp.sum(p, axis=1, keepdims=True)
                p_ref[...] = p.astype(BF16)
        else:
            c = i - n_chunks
            for hp in range(N_HEADS // 2):
                vpair = jnp.concatenate(
                    [vbuf_ref[slot, pl.ds(2 * hp + d, chunk, stride=N_HEADS), :].astype(BF16) for d in range(2)],
                    axis=1)
                ppair = p_ref[2 * hp * rows_per_head:(2 * hp + 2) * rows_per_head, c * chunk:(c + 1) * chunk]
                acc[hp] = acc[hp] + jnp.dot(ppair, vpair, preferred_element_type=F32)
    lam = _lambda(lq1_ref, lk1_ref, lq2_ref, lk2_ref, lam_init)
    outs = []
    for h in range(N_HEADS):
        rows = slice(h * rows_per_head, (h + 1) * rows_per_head)
        d = h % 2
        a = acc[h // 2][d * rows_per_head:(d + 1) * rows_per_head, d * HEAD_W:(d + 1) * HEAD_W]
        a = a + jnp.dot(p_ref[rows, past:], vn[:, h * HEAD_W:(h + 1) * HEAD_W], preferred_element_type=F32)
        on = a / l[rows, :]
        outs.append(_head_norm(on[:L, :] - lam * on[L:, :], g_ref[...], lam_init))
    o_ref[...] = jnp.concatenate(outs, axis=1)
    y_ref[...] = _mlp_back(x1_ref[...], ff_ref[...], g2_ref[...], nf_ref[...], final_norm)


def _sample_attention_prompt_mlp(q, k_new, v_new, ck, cv, page_table, lams, subln_g, lam_init,
                                 x, y_pool, o, mod, norm2_g, final_g, w_out, w1, w2, final_norm):
    B, L, W = q.shape
    PB, S, D = x.shape
    n_pages = page_table.shape[1]
    page_rows = ck.shape[2]
    chunk = PAGES_PER_CHUNK * page_rows
    past = n_pages * page_rows
    n_rows = 2 * N_HEADS * L
    TR = PB * S // B
    assert TR * B == PB * S and S % TR == 0 and TR % 8 == 0
    assert n_pages % PAGES_PER_CHUNK == 0 and (2 * n_pages // PAGES_PER_CHUNK) % RING_SLOTS == 0
    new_blk = pl.BlockSpec((None, L, W), lambda b, pt: (b, 0, 0))
    lam_blk = pl.BlockSpec((1, HEAD_DIM), lambda b, pt: (0, 0))
    row_blk = lambda w: pl.BlockSpec((TR, w), lambda b, pt: (b, 0))
    mod_blk = lambda j: pl.BlockSpec((None, 1, D), lambda b, pt: (b * TR // S, 0, j))
    grid_spec = pltpu.PrefetchScalarGridSpec(
        num_scalar_prefetch=1,
        grid=(B,),
        in_specs=[new_blk, new_blk, new_blk, lam_blk, lam_blk, lam_blk, lam_blk,
                  pl.BlockSpec((1, HEAD_W), lambda b, pt: (0, 0)),
                  pl.BlockSpec(memory_space=pl.ANY), pl.BlockSpec(memory_space=pl.ANY),
                  row_blk(D), row_blk(y_pool.shape[2]), row_blk(o.shape[2]),
                  mod_blk(2), mod_blk(3), mod_blk(4), mod_blk(5),
                  _resident((1, D)), _resident((1, D)), _resident(w_out.shape), _resident(w1.shape),
                  _resident(w2.shape)],
        out_specs=[new_blk, row_blk(D)],
        scratch_shapes=[pltpu.VMEM((RING_SLOTS, W, chunk), ck.dtype),
                        pltpu.VMEM((RING_SLOTS, chunk * N_HEADS, HEAD_W), cv.dtype),
                        pltpu.VMEM((n_rows, past + LANES), F32),
                        pltpu.VMEM((n_rows, past + LANES), BF16),
                        pltpu.SemaphoreType.DMA((RING_SLOTS,)),
                        pltpu.SemaphoreType.DMA((RING_SLOTS,)),
                        pltpu.VMEM((TR, D), F32),
                        pltpu.VMEM((TR, D), BF16),
                        pltpu.VMEM((TR, FF_CHUNK), BF16),
                        pltpu.VMEM((TR, D), F32)])
    kern = functools.partial(_sample_attn_mlp_kernel, n_pages=n_pages, lam_init=lam_init, final_norm=final_norm)
    o_s, y = pl.pallas_call(
        kern, grid_spec=grid_spec,
        out_shape=[jax.ShapeDtypeStruct((B, L, W), F32), jax.ShapeDtypeStruct((PB * S, D), F32)],
        compiler_params=_params(1), name="sample_attention_prompt_mlp")(
        page_table.reshape(-1), q, k_new, v_new, *[t.reshape(1, HEAD_DIM) for t in lams],
        subln_g.reshape(1, HEAD_W), ck, cv,
        x.reshape(PB * S, D), y_pool.reshape(PB * S, -1), o.reshape(PB * S, -1), mod, mod, mod, mod,
        norm2_g.reshape(1, D), final_g.reshape(1, D), w_out, w1, w2)
    return o_s, y.reshape(PB, S, D)


def _mlp_front(x, yp, o, g1, sh2, sc2, n2, wo_ref):
    G, R, D = x.shape
    cat = jnp.concatenate([yp.reshape(G * R, -1).astype(BF16), o.reshape(G * R, -1).astype(BF16)], axis=1)
    mix = jnp.dot(cat, wo_ref[...], preferred_element_type=F32)
    x = x + g1 * mix.reshape(G, R, D)
    h = (_rms(x, n2) * (1.0 + sc2) + sh2).reshape(G * R, D).astype(BF16)
    return x, h


def _mlp_up(h, w1_ref, c):
    f = jnp.dot(h, w1_ref[:, c * FF_CHUNK:(c + 1) * FF_CHUNK], preferred_element_type=F32)
    return jnp.square(jnp.maximum(f, 0.0)).astype(BF16)


def _mlp_down(f, w2_ref, c):
    return jnp.dot(f, w2_ref[c * FF_CHUNK:(c + 1) * FF_CHUNK, :], preferred_element_type=F32)


def _mlp_back(x, acc, g2, nf, final_norm):
    x = x + g2 * acc.reshape(x.shape)
    return _rms(x, nf) if final_norm else x


def _mlp_kernel(x_ref, yp_ref, o_ref, g1_ref, sh2_ref, sc2_ref, g2_ref, n2_ref, nf_ref, wo_ref, w1_ref, w2_ref,
                y_ref, *, final_norm):
    x, h = _mlp_front(x_ref[...], yp_ref[...], o_ref[...], g1_ref[...], sh2_ref[...], sc2_ref[...], n2_ref[...],
                      wo_ref)
    acc = jnp.zeros(h.shape, F32)
    for c in range(w1_ref.shape[1] // FF_CHUNK):
        acc = acc + _mlp_down(_mlp_up(h, w1_ref, c), w2_ref, c)
    y_ref[...] = _mlp_back(x, acc, g2_ref[...], nf_ref[...], final_norm)


def _mlp(x, y_pool, o, mod, norm2_g, final_g, w_out, w1, w2, *, group_rows, final_norm):
    B, L, D = x.shape
    if L >= group_rows:
        G, R = 1, group_rows
    else:
        G, R = group_rows // L, L
    grid = (B // G, L // R)
    blk = lambda w: pl.BlockSpec((G, R, w), lambda b, l: (b, l, 0))
    mod_blk = lambda j: pl.BlockSpec((G, 1, D), lambda b, l: (b, 0, j))
    kern = functools.partial(_mlp_kernel, final_norm=final_norm)
    return pl.pallas_call(
        kern, grid=grid,
        in_specs=[blk(D), blk(y_pool.shape[2]), blk(o.shape[2]), mod_blk(2), mod_blk(3), mod_blk(4), mod_blk(5),
                  _resident((1, D)), _resident((1, D)), _resident(w_out.shape), _resident(w1.shape),
                  _resident(w2.shape)],
        out_specs=blk(D),
        out_shape=jax.ShapeDtypeStruct((B, L, D), F32),
        compiler_params=_params(2), name="mlp")(
        x, y_pool, o, mod, mod, mod, mod, norm2_g.reshape(1, D), final_g.reshape(1, D), w_out, w1, w2)


def _rope_angles(pos):
    inv_freq = ROPE_THETA ** (-jnp.arange(HALF, dtype=F32) * 2.0 / HEAD_DIM)
    ang = pos.astype(F32)[:, None] * inv_freq[None, :]
    return jnp.cos(ang), jnp.sin(ang)


def _rope_row_tables(cos, sin):
    reps = LANES // HEAD_DIM
    return jnp.tile(cos, (1, 2 * reps)), jnp.tile(jnp.concatenate([-sin, sin], axis=1), (1, reps))


def _block_diag(w):
    g, c, d = w.shape
    out = jnp.zeros((g * c, g * d), w.dtype)
    for i in range(g):
        out = out.at[i * c:(i + 1) * c, i * d:(i + 1) * d].set(w[i])
    return out


def kernel(x_prompt, x_sample, c_prompt, c_sample, cache_k, cache_v, state_pool, page_table, w_ada, b_ada, norm1_g, norm2_g, w_in, pool_w, pool_scale, lambda_q1, lambda_k1, lambda_q2, lambda_k2, subln_g, w_out, w_mlp1, w_mlp2, final_g):
    depth = w_ada.shape[0]
    B, S, D = x_prompt.shape
    NB, L, _ = x_sample.shape
    n_pages = page_table.shape[1]
    n_pool_pages, page_rows = cache_k.shape[1], cache_k.shape[2]
    past_len = n_pages * page_rows
    attn_w = cache_k.shape[3] * cache_k.shape[4]
    pool_cols = pool_scale.shape[1]

    cos_p, sin_p = _rope_angles(jnp.arange(S, dtype=jnp.int32))
    cos_s, sin_s = _rope_row_tables(*_rope_angles(past_len + jnp.arange(L, dtype=jnp.int32)))
    cos_s, sin_s = jnp.tile(cos_s, (NB, 1)), jnp.tile(sin_s, (NB, 1))
    c_all = jnp.concatenate([c_prompt, c_sample], axis=0)

    xp, xs = x_prompt, x_sample
    kp_l, vp_l, pp_l, ks_l, vs_l, ps_l = [], [], [], [], [], []
    for l in range(depth):
        lam_init = 0.8 - 0.6 * math.exp(-0.3 * l)
        lams = (lambda_q1[l], lambda_k1[l], lambda_q2[l], lambda_k2[l])
        w_pool = w_in[l][:, :pool_cols].astype(BF16)
        w_qkv_t = w_in[l][:, pool_cols:].T.astype(BF16)
        w_out_b, w1_b, w2_b = w_out[l].astype(BF16), w_mlp1[l].astype(BF16), w_mlp2[l].astype(BF16)
        pool_bd = _block_diag(pool_w[l]).astype(BF16)
        last = l == depth - 1

        mod = _modulation(c_all, w_ada[l], b_ada[l])
        mod_p = mod[:B].reshape(B, 1, N_MOD * D)
        mod_s = mod[B:].reshape(NB, 1, N_MOD * D)

        kt, v, qtb, kb, vtb, yp, tail = _inproj_prompt(xp, mod_p, norm1_g[l], w_pool, w_qkv_t, cos_p.T, sin_p.T,
                                                       pool_bd, pool_scale[l])
        o_p = _prompt_attention(qtb, kb, vtb, lams, subln_g[l], lam_init)
        kp_l.append(kt.reshape(B, 2 * N_HEADS, HEAD_DIM, S).transpose(0, 3, 1, 2))
        vp_l.append(v.reshape(B, S, N_HEADS, HEAD_W))
        pp_l.append(tail[:, POOL_PRE - POOL_BUF:])

        pre = jnp.pad(state_pool[l], ((0, 0), (POOL_PRE - POOL_BUF, 0), (0, 0)))
        k, v, qf, yp_s, tail = _inproj_sample(xs, mod_s, norm1_g[l], w_pool, w_qkv_t, cos_s, sin_s, pool_bd,
                                              pool_scale[l], pre, pos0=past_len)
        ck = cache_k[l].transpose(0, 2, 3, 1).reshape(n_pool_pages, attn_w, page_rows)
        cv = cache_v[l].reshape(n_pool_pages, page_rows * N_HEADS, HEAD_W)
        o_s, xp = _sample_attention_prompt_mlp(qf, k, v, ck, cv, page_table, lams, subln_g[l], lam_init,
                                               xp, yp, o_p, mod_p, norm2_g[l], final_g, w_out_b, w1_b, w2_b, last)
        ks_l.append(k.reshape(NB, L, 2 * N_HEADS, HEAD_DIM))
        vs_l.append(v.reshape(NB, L, N_HEADS, HEAD_W))
        ps_l.append(tail[:, POOL_PRE - POOL_BUF:])
        xs = _mlp(xs, yp_s, o_s, mod_s, norm2_g[l], final_g, w_out_b, w1_b, w2_b, group_rows=TOKEN_TILE,
                  final_norm=last)

    return (xp, xs, jnp.stack(kp_l), jnp.stack(vp_l), jnp.stack(pp_l),
            jnp.stack(ks_l), jnp.stack(vs_l), jnp.stack(ps_l))
```
